```python
import jax, jax.numpy as jnp
from jax import lax
import numpy as np

D_MODEL = 1024
BATCH = 2
SEQ = 16384
DEPTH = 4

N_META = 16
N_MIXERS = 4
D_MIX = D_MODEL
GROUP_W = D_MIX // N_MIXERS
HEAD = 64
N_H = GROUP_W // HEAD
POOL_WINDOWS = (2, 4, 8, 16)
RW_W_RANK = 32
RW_A_RANK = 32
RW_V_RANK = 32
RW_G_RANK = 64
RW_COLS = 3 * GROUP_W + RW_W_RANK + RW_A_RANK + RW_G_RANK
RW_SPLITS = (GROUP_W, 2 * GROUP_W, 3 * GROUP_W, 3 * GROUP_W + RW_W_RANK, 3 * GROUP_W + RW_W_RANK + RW_A_RANK)
LRU_C = 8.0
CONV_W = 4
ML_CHUNK = 64
ML_COLS = 4 * GROUP_W + 2 * N_H
PROJ_SPLITS = (GROUP_W, GROUP_W + RW_COLS, 2 * GROUP_W + RW_COLS, 3 * GROUP_W + RW_COLS)
D_IN = 3 * GROUP_W + RW_COLS + ML_COLS
D_FF = 4 * D_MODEL
ALPHA = (2 * DEPTH) ** 0.25
BETA = (8 * DEPTH) ** -0.25
LN_EPS = 1e-5
GN_EPS = 64e-5
NEG = -1e30

kernel_name = 'hybrid_parallel_heads_pool_rwkv7_rglru_mlstm'


def layer_norm(x, g, b):
    xf = x.astype(jnp.float32)
    mu = jnp.mean(xf, -1, keepdims=True)
    var = jnp.mean(jnp.square(xf - mu), -1, keepdims=True)
    return ((xf - mu) * lax.rsqrt(var + LN_EPS) * g + b).astype(x.dtype)


def head_norm(y, g, b):
    mu = jnp.mean(y, -1, keepdims=True)
    var = jnp.mean(jnp.square(y - mu), -1, keepdims=True)
    yn = (y - mu) * lax.rsqrt(var + GN_EPS)
    return yn.reshape(y.shape[0], y.shape[1], -1) * g + b


def split_heads(z):
    return z.reshape(z.shape[0], z.shape[1], N_H, HEAD).astype(jnp.float32)


def token_shift(z):
    return jnp.pad(z, ((0, 0), (1, 0), (0, 0)))[:, :-1]


def causal_dwconv(z, w, b):
    y = lax.conv_general_dilated(z, w[:, None, :].astype(z.dtype), window_strides=(1,),
                                 padding=[(CONV_W - 1, 0)], dimension_numbers=('NWC', 'WIO', 'NWC'),
                                 feature_group_count=z.shape[-1])
    return y + b


def pool_mixer(u, w_blk, scale):
    B, T, _ = u.shape
    uf = u.astype(jnp.float32)
    cs = jnp.pad(jnp.cumsum(uf, axis=1), ((0, 0), (1, 0), (0, 0)))
    pos = jnp.arange(T, dtype=jnp.float32)
    groups = []
    for gi, w in enumerate(POOL_WINDOWS):
        c = cs[:, :, gi * HEAD:(gi + 1) * HEAD]
        prev = jnp.pad(c, ((0, 0), (w - 1, 0), (0, 0)))[:, :T]
        mean = (c[:, 1:] - prev) / jnp.minimum(pos + 1.0, float(w))[None, :, None]
        groups.append(mean - uf[:, :, gi * HEAD:(gi + 1) * HEAD])
    d = jnp.stack(groups, axis=2)
    y = jnp.einsum('btgc,gcd->btgd', d, w_blk).reshape(B, T, GROUP_W) * scale
    return y.astype(u.dtype)


def rwkv7_mixer(p, mu, w0, w_up, a0, a_up, g_up, k_k, k_a, r_k, gn_g, gn_b, v_first, v_mix):
    B, T, _ = p.shape
    pm = p + (token_shift(p) - p) * mu
    r, k, v, wd, ad, gd = jnp.split(pm, RW_SPLITS, axis=-1)
    w = -jax.nn.softplus(-(w0 + jnp.tanh(wd) @ w_up)) - 0.5
    log_decay = -jnp.exp(w.astype(jnp.float32))
    a = jax.nn.sigmoid(a0 + ad @ a_up)
    g = jax.nn.sigmoid(gd) @ g_up
    if v_first is None:
        v_first = v
    else:
        v0, v_down, v_up = v_mix
        v = v + (v_first - v) * jax.nn.sigmoid(v0 + (v @ v_down) @ v_up)
    kk = split_heads(k * k_k)
    kk = kk * lax.rsqrt(jnp.sum(kk * kk, -1, keepdims=True) + 1e-12)
    k = k * (1.0 + (a - 1.0) * k_a)
    rh, kh, vh, ah = split_heads(r), split_heads(k), split_heads(v), split_heads(a)
    dh = jnp.exp(split_heads(log_decay))
    bvec = kk * ah
    xs = tuple(jnp.moveaxis(z, 1, 0) for z in (rh, dh, kh, vh, kk, bvec))

    def step(S, inp):
        r_t, d_t, k_t, v_t, kk_t, b_t = inp
        sa = jnp.einsum('bhvk,bhk->bhv', S, kk_t)
        S = S * d_t[:, :, None, :] - sa[..., None] * b_t[:, :, None, :] + v_t[..., None] * k_t[:, :, None, :]
        return S, jnp.einsum('bhvk,bhk->bhv', S, r_t)

    S0 = jnp.zeros((B, N_H, HEAD, HEAD), jnp.float32)
    _, y = lax.scan(step, S0, xs)
    y = head_norm(jnp.moveaxis(y, 0, 1), gn_g, gn_b)
    bonus = jnp.sum(rh * kh * r_k, -1, keepdims=True) * vh
    out = (y + bonus.reshape(B, T, GROUP_W)) * g
    return out.astype(p.dtype), v_first


def rglru_mixer(xb, gate, conv_w, conv_b, ga_w, ga_b, gx_w, gx_b, lam):
    B, T, _ = xb.shape
    xc = causal_dwconv(xb, conv_w, conv_b)
    xh = split_heads(xc)
    r = jax.nn.sigmoid(jnp.einsum('btgc,gcd->btgd', xh, ga_w).reshape(B, T, GROUP_W) + ga_b)
    i = jax.nn.sigmoid(jnp.einsum('btgc,gcd->btgd', xh, gx_w).reshape(B, T, GROUP_W) + gx_b)
    log_a = -LRU_C * r * jax.nn.softplus(-lam.astype(jnp.float32))
    a = jnp.exp(log_a)
    u = jnp.sqrt(-jnp.expm1(2.0 * log_a)) * (i * xc.astype(jnp.float32))

    def combine(lhs, rhs):
        return lhs[0] * rhs[0], rhs[0] * lhs[1] + rhs[1]

    _, h = lax.associative_scan(combine, (a, u), axis=1)
    return (h * jax.nn.gelu(gate.astype(jnp.float32))).astype(xb.dtype)


def mlstm_mixer(p, if_b, gn_g, gn_b):
    B, T, _ = p.shape
    L = ML_CHUNK
    pad = L - N_META
    Tp = T + pad
    NC = Tp // L
    pp = jnp.pad(p.astype(jnp.float32), ((0, 0), (pad, 0), (0, 0)))
    q, k, v, o, gates = jnp.split(pp, (GROUP_W, 2 * GROUP_W, 3 * GROUP_W, 4 * GROUP_W), axis=-1)

    def chunks(z):
        return z.reshape(B, NC, L, N_H, HEAD)

    q = chunks(q) * HEAD ** -0.5
    k = chunks(k)
    v = chunks(v)
    gates = (gates + if_b).reshape(B, NC, L, 2 * N_H)
    valid = (jnp.arange(Tp) >= pad).reshape(1, NC, L, 1)
    logi = jnp.where(valid, gates[..., :N_H], NEG)
    logf = jnp.where(valid, jax.nn.log_sigmoid(gates[..., N_H:]), 0.0)
    b = jnp.cumsum(logf, axis=2)
    causal = jnp.tril(jnp.ones((L, L), dtype=bool))[None, None, :, :, None]
    dmat = jnp.where(causal, b[:, :, :, None, :] - b[:, :, None, :, :] + logi[:, :, None, :, :], NEG)
    m_intra = jnp.max(dmat, axis=3)
    b_last = b[:, :, -1]
    g_loc = b_last[:, :, None] - b + logi
    m_loc = jnp.max(g_loc, axis=2)
    w_loc = jnp.exp(g_loc - m_loc[:, :, None])
    c_loc = jnp.einsum('bnlh,bnlhv,bnlhk->bnhvk', w_loc, v, k)
    n_loc = jnp.einsum('bnlh,bnlhk->bnhk', w_loc, k)

    def step(carry, inp):
        c_st, n_st, m_st = carry
        c_l, n_l, m_l, bl = inp
        m_new = jnp.maximum(bl + m_st, m_l)
        s_old = jnp.exp(bl + m_st - m_new)
        s_new = jnp.exp(m_l - m_new)
        c_new = s_old[..., None, None] * c_st + s_new[..., None, None] * c_l
        n_new = s_old[..., None] * n_st + s_new[..., None] * n_l
        return (c_new, n_new, m_new), (c_st, n_st, m_st)

    init = (jnp.zeros((B, N_H, HEAD, HEAD), jnp.float32), jnp.zeros((B, N_H, HEAD), jnp.float32),
            jnp.zeros((B, N_H), jnp.float32))
    xs = tuple(jnp.moveaxis(z, 1, 0) for z in (c_loc, n_loc, m_loc, b_last))
    _, (c_prev, n_prev, m_prev) = lax.scan(step, init, xs)
    c_prev = jnp.moveaxis(c_prev, 0, 1)
    n_prev = jnp.moveaxis(n_prev, 0, 1)
    m_prev = jnp.moveaxis(m_prev, 0, 1)
    m_inter = b + m_prev[:, :, None, :]
    m_t = jnp.maximum(m_intra, m_inter)
    s = jnp.einsum('bnthd,bnshd->bntsh', q, k) * jnp.exp(dmat - m_t[:, :, :, None, :])
    w_inter = jnp.exp(m_inter - m_t)
    num = jnp.einsum('bntsh,bnshv->bnthv', s, v) + w_inter[..., None] * jnp.einsum('bnhvk,bnthk->bnthv', c_prev, q)
    den = jnp.sum(s, axis=3) + w_inter * jnp.einsum('bnhk,bnthk->bnth', n_prev, q)
    hh = num / jnp.maximum(jnp.abs(den), jnp.exp(-m_t))[..., None]
    hh = hh.reshape(B, Tp, N_H, HEAD)[:, pad:]
    y = head_norm(hh, gn_g, gn_b) * jax.nn.sigmoid(o[:, pad:])
    return y.astype(p.dtype)


def setup_inputs(seed: int = 0) -> dict:
    key = jax.random.key(seed)
    ks = iter(jax.random.split(key, 64))

    def nrm(shape, scale):
        return scale * jax.random.normal(next(ks), shape, jnp.float32)

    def gain(shape):
        return 1.0 + nrm(shape, 0.02)

    dm1 = DEPTH - 1
    sig = jax.random.uniform(next(ks), (DEPTH, GROUP_W), jnp.float32, 0.9, 0.999) ** (1.0 / LRU_C)
    lru_lambda = jnp.log(sig) - jnp.log1p(-sig)
    ml_if_b = jnp.concatenate([nrm((DEPTH, N_H), 0.1) - 1.0,
                               jnp.broadcast_to(jnp.linspace(3.0, 6.0, N_H), (DEPTH, N_H)) + nrm((DEPTH, N_H), 0.1)], axis=-1)
    return {
        'x': nrm((BATCH, SEQ, D_MODEL), 1.0),
        'meta': nrm((N_META, D_MODEL), 1.0),
        'emb_ln_g': gain((D_MODEL,)),
        'emb_ln_b': nrm((D_MODEL,), 0.02),
        'w_in': nrm((DEPTH, D_MODEL, D_IN), D_MODEL ** -0.5),
        'w_out': nrm((DEPTH, D_MIX, D_MODEL), BETA * D_MIX ** -0.5),
        'pool_w': nrm((DEPTH, N_H, HEAD, HEAD), HEAD ** -0.5),
        'pool_scale': gain((DEPTH, GROUP_W)),
        'rw_mu': jax.random.uniform(next(ks), (DEPTH, RW_COLS), jnp.float32),
        'rw_w0': jax.random.uniform(next(ks), (DEPTH, GROUP_W), jnp.float32, -6.0, -1.0),
        'rw_w_up': nrm((DEPTH, RW_W_RANK, GROUP_W), 0.1),
        'rw_a0': nrm((DEPTH, GROUP_W), 0.1),
        'rw_a_up': nrm((DEPTH, RW_A_RANK, GROUP_W), RW_A_RANK ** -0.5),
        'rw_g_up': nrm((DEPTH, RW_G_RANK, GROUP_W), RW_G_RANK ** -0.5),
        'rw_k_k': 0.85 + nrm((DEPTH, GROUP_W), 0.05),
        'rw_k_a': 1.0 + nrm((DEPTH, GROUP_W), 0.05),
        'rw_r_k': nrm((DEPTH, N_H, HEAD), 0.1),
        'rw_gn_g': gain((DEPTH, GROUP_W)),
        'rw_gn_b': nrm((DEPTH, GROUP_W), 0.02),
        'rw_v0': nrm((dm1, GROUP_W), 0.1),
        'rw_v_down': nrm((dm1, GROUP_W, RW_V_RANK), GROUP_W ** -0.5),
        'rw_v_up': nrm((dm1, RW_V_RANK, GROUP_W), 0.1),
        'lru_conv_w': nrm((DEPTH, CONV_W, GROUP_W), CONV_W ** -0.5),
        'lru_conv_b': nrm((DEPTH, GROUP_W), 0.02),
        'lru_ga_w': nrm((DEPTH, N_H, HEAD, HEAD), HEAD ** -0.5),
        'lru_ga_b': nrm((DEPTH, GROUP_W), 0.1),
        'lru_gx_w': nrm((DEPTH, N_H, HEAD, HEAD), HEAD ** -0.5),
        'lru_gx_b': nrm((DEPTH, GROUP_W), 0.1),
        'lru_lambda': lru_lambda,
        'ml_if_b': ml_if_b,
        'ml_gn_g': gain((DEPTH, GROUP_W)),
        'ml_gn_b': nrm((DEPTH, GROUP_W), 0.02),
        'ln1_g': gain((DEPTH, D_MODEL)),
        'ln1_b': nrm((DEPTH, D_MODEL), 0.02),
        'ln2_g': gain((DEPTH, D_MODEL)),
        'ln2_b': nrm((DEPTH, D_MODEL), 0.02),
        'mlp_w1': nrm((DEPTH, D_MODEL, D_FF), D_MODEL ** -0.5),
        'mlp_w2': nrm((DEPTH, D_FF, D_MODEL), BETA * D_FF ** -0.5),
    }


def reference(x, meta, emb_ln_g, emb_ln_b, w_in, w_out, pool_w, pool_scale, rw_mu, rw_w0, rw_w_up,
              rw_a0, rw_a_up, rw_g_up, rw_k_k, rw_k_a, rw_r_k, rw_gn_g, rw_gn_b, rw_v0, rw_v_down,
              rw_v_up, lru_conv_w, lru_conv_b, lru_ga_w, lru_ga_b, lru_gx_w, lru_gx_b, lru_lambda,
              ml_if_b, ml_gn_g, ml_gn_b, ln1_g, ln1_b, ln2_g, ln2_b, mlp_w1, mlp_w2):
    B = x.shape[0]
    h = jnp.concatenate([jnp.broadcast_to(meta[None].astype(x.dtype), (B, N_META, D_MODEL)), x], axis=1)
    h = layer_norm(h, emb_ln_g, emb_ln_b)
    v_first = None
    for l in range(DEPTH):
        p = h @ w_in[l]
        p_pool, p_rw, p_lru_x, p_lru_g, p_ml = jnp.split(p, PROJ_SPLITS, axis=-1)
        y_pool = pool_mixer(p_pool, pool_w[l], pool_scale[l])
        v_mix = None if l == 0 else (rw_v0[l - 1], rw_v_down[l - 1], rw_v_up[l - 1])
        y_rw, v_first = rwkv7_mixer(p_rw, rw_mu[l], rw_w0[l], rw_w_up[l], rw_a0[l], rw_a_up[l], rw_g_up[l],
                                    rw_k_k[l], rw_k_a[l], rw_r_k[l], rw_gn_g[l], rw_gn_b[l], v_first, v_mix)
        y_lru = rglru_mixer(p_lru_x, p_lru_g, lru_conv_w[l], lru_conv_b[l], lru_ga_w[l], lru_ga_b[l],
                            lru_gx_w[l], lru_gx_b[l], lru_lambda[l])
        y_ml = mlstm_mixer(p_ml, ml_if_b[l], ml_gn_g[l], ml_gn_b[l])
        mix = jnp.concatenate([y_pool, y_rw, y_lru, y_ml], axis=-1) @ w_out[l]
        h = layer_norm(ALPHA * h + mix, ln1_g[l], ln1_b[l])
        ff = jnp.square(jax.nn.relu(h @ mlp_w1[l])) @ mlp_w2[l]
        h = layer_norm(ALPHA * h + ff, ln2_g[l], ln2_b[l])
    return h[:, N_META:]
```

```python
import functools
import math

import jax
import jax.numpy as jnp
from jax import lax
from jax.experimental import pallas as pl
from jax.experimental.pallas import tpu as pltpu

D_MODEL = 1024
N_META = 16
GROUP_W = 256
HEAD = 64
N_H = 4
POOL_MAX_W = 16
CONV_W = 4
LRU_C = 8.0
LN_EPS = 1e-5
GN_EPS = 64e-5
NEG = -1e30
D_FF = 4096
RW_COLS = 896
ML_COLS_PAD = 1152
D_IN_PAD = 2816
LANE = 128
SUBLANE = 8

TB = 256
RW_CHUNK = 64
TM = 512
VMEM_LIMIT = 56 * 1024 * 1024

F32 = jnp.float32
BF16 = jnp.bfloat16

_NN = (((1,), (0,)), ((), ()))
_NT = (((1,), (1,)), ((), ()))
_TN = (((0,), (0,)), ((), ()))


def _dot(a, b, dims=_NN):
    return lax.dot_general(a, b, dims, preferred_element_type=F32)


def _bdot(a, b, dims=_NN):
    return _dot(a.astype(BF16), b.astype(BF16), dims)


def _split3(x):
    hi = x.astype(BF16)
    r1 = x - hi.astype(F32)
    mid = r1.astype(BF16)
    lo = (r1 - mid.astype(F32)).astype(BF16)
    return hi, mid, lo


def _dot_exact_lhs(e_bf16, x, dims=_NN):
    hi, mid, lo = _split3(x)
    return _dot(e_bf16, hi, dims) + _dot(e_bf16, mid, dims) + _dot(e_bf16, lo, dims)


def _softplus(x):
    return jnp.maximum(x, 0.0) + jnp.log1p(jnp.exp(-jnp.abs(x)))


def _layer_norm(x, g, b):
    mu = jnp.mean(x, axis=-1, keepdims=True)
    xc = x - mu
    var = jnp.mean(xc * xc, axis=-1, keepdims=True)
    return xc * lax.rsqrt(var + LN_EPS) * g + b


def _head_masks(width=GROUP_W):
    lane = lax.broadcasted_iota(jnp.int32, (1, width), 1)
    return [(lane // HEAD) == h for h in range(N_H)]


def _head_sum(x, hms):
    out = jnp.zeros_like(x)
    for hm in hms:
        s = jnp.sum(jnp.where(hm, x, 0.0), axis=1, keepdims=True)
        out = out + jnp.where(hm, s, 0.0)
    return out


def _head_norm(y, g, b, hms):
    mu = _head_sum(y, hms) * (1.0 / HEAD)
    yc = y - mu
    var = _head_sum(yc * yc, hms) * (1.0 / HEAD)
    return yc * lax.rsqrt(var + GN_EPS) * g + b


def _col(x, j):
    lane = lax.broadcasted_iota(jnp.int32, (1, x.shape[1]), 1)
    return jnp.sum(jnp.where(lane == j, x, 0.0), axis=1, keepdims=True)


def _valid_rows(t):
    row = t * TB + lax.broadcasted_iota(jnp.int32, (TB, 1), 0)
    return row, row >= (TB - N_META)


def _embed_ln_kernel(x_ref, g_ref, b_ref, o_ref):
    o_ref[...] = _layer_norm(x_ref[...], g_ref[...], b_ref[...])


def _inproj_kernel(h_ref, w_ref, o_pool, o_rw, o_lru, o_ml):
    hb = h_ref[...].astype(BF16)
    o_pool[...] = _dot(hb, w_ref[:, 0:256])
    o_rw[...] = _dot(hb, w_ref[:, 256:1152])
    o_lru[...] = _dot(hb, w_ref[:, 1152:1664])
    o_ml[...] = _dot(hb, w_ref[:, 1664:D_IN_PAD])


def _outproj_kernel(alpha, h_ref, y0, y1, y2, y3, w_ref, g_ref, b_ref, o_ref):
    mix = _dot(y0[...].astype(BF16), w_ref[0:256, :])
    mix += _dot(y1[...].astype(BF16), w_ref[256:512, :])
    mix += _dot(y2[...].astype(BF16), w_ref[512:768, :])
    mix += _dot(y3[...].astype(BF16), w_ref[768:1024, :])
    o_ref[...] = _layer_norm(alpha * h_ref[...] + mix, g_ref[...], b_ref[...])


def _mlp_kernel(alpha, h_ref, w1_ref, w2_ref, g_ref, b_ref, o_ref):
    h = h_ref[...]
    hb = h.astype(BF16)
    acc = jnp.zeros_like(h)
    for j in range(D_FF // D_MODEL):
        cs = slice(j * D_MODEL, (j + 1) * D_MODEL)
        z = jnp.maximum(_dot(hb, w1_ref[:, cs]), 0.0)
        acc += _dot((z * z).astype(BF16), w2_ref[cs, :])
    o_ref[...] = _layer_norm(alpha * h + acc, g_ref[...], b_ref[...])


def _row_spec(width):
    return pl.BlockSpec((TM, width), lambda i: (i, 0))


def _const_spec(shape):
    return pl.BlockSpec(shape, lambda i: (0,) * len(shape))


def _rows_call(kernel, name, rows, row_ins, const_ins, out_widths):
    n = pl.cdiv(rows, TM)
    in_specs = [_row_spec(a.shape[1]) for a in row_ins] + [_const_spec(a.shape) for a in const_ins]
    out_shape = [jax.ShapeDtypeStruct((rows, w), F32) for w in out_widths]
    out_specs = [_row_spec(w) for w in out_widths]
    single = len(out_widths) == 1
    res = pl.pallas_call(
        kernel,
        grid=(n,),
        in_specs=in_specs,
        out_specs=out_specs[0] if single else out_specs,
        out_shape=out_shape[0] if single else out_shape,
        compiler_params=pltpu.CompilerParams(dimension_semantics=("arbitrary",), vmem_limit_bytes=VMEM_LIMIT),
        name=name,
    )(*row_ins, *const_ins)
    return res


_SCAN_PAD = TB // 2


def _pool_lru_kernel(pp_ref, plru_ref, poolw_ref, pscale_ref, convw_ref, convb_ref, wax_ref, bax_ref,
                     lam_ref, ypool_ref, ylru_ref, pext, cext, s_a, s_u, hcar):
    t = pl.program_id(1)

    @pl.when(t == 0)
    def _():
        pext[0:POOL_MAX_W, :] = jnp.zeros((POOL_MAX_W, GROUP_W), F32)
        cext[0:SUBLANE, :] = jnp.zeros((SUBLANE, GROUP_W), F32)
        hcar[...] = jnp.zeros((1, GROUP_W), F32)
        s_a[0:_SCAN_PAD, :] = jnp.ones((_SCAN_PAD, GROUP_W), F32)
        s_u[0:_SCAN_PAD, :] = jnp.zeros((_SCAN_PAD, GROUP_W), F32)

    row, valid = _valid_rows(t)

    u = jnp.where(valid, pp_ref[0], 0.0)
    pext[POOL_MAX_W:POOL_MAX_W + TB, :] = u
    lane = lax.broadcasted_iota(jnp.int32, (1, GROUP_W), 1)
    win = jnp.left_shift(2, lane // HEAD)
    acc = jnp.zeros((TB, GROUP_W), F32)
    for j in range(POOL_MAX_W):
        acc = acc + jnp.where(win > j, pext[POOL_MAX_W - j:POOL_MAX_W - j + TB, :], 0.0)
    pos1 = (row - (TB - N_META) + 1).astype(F32)
    cnt = jnp.clip(pos1, 1.0, win.astype(F32))
    dlt = acc / cnt - u
    ypool_ref[0] = _bdot(dlt, poolw_ref[...]) * pscale_ref[...]
    pext[0:POOL_MAX_W, :] = pext[TB:TB + POOL_MAX_W, :]

    x = jnp.where(valid, plru_ref[0, :, 0:GROUP_W], 0.0)
    gate = plru_ref[0, :, GROUP_W:2 * GROUP_W]
    cext[SUBLANE:SUBLANE + TB, :] = x
    xc = jnp.zeros((TB, GROUP_W), F32) + convb_ref[...]
    for j in range(CONV_W):
        off = SUBLANE - (CONV_W - 1) + j
        xc = xc + convw_ref[j:j + 1, :] * cext[off:off + TB, :]
    cext[0:SUBLANE, :] = cext[TB:TB + SUBLANE, :]
    z = _bdot(xc, wax_ref[...]) + bax_ref[...]
    r = jax.nn.sigmoid(z[:, 0:GROUP_W])
    i = jax.nn.sigmoid(z[:, GROUP_W:2 * GROUP_W])
    log_a = (-LRU_C) * r * _softplus(-lam_ref[...])
    a = jnp.exp(log_a)
    uu = jnp.where(valid, jnp.sqrt(-jnp.tanh(log_a) * (a * a + 1.0)) * (i * xc), 0.0)
    lo, hi = _SCAN_PAD, _SCAN_PAD + TB
    s_a[lo:hi, :] = a
    s_u[lo:hi, :] = uu
    d = 1
    while d < TB:
        a0, u0 = s_a[lo:hi, :], s_u[lo:hi, :]
        a1, u1 = s_a[lo - d:hi - d, :], s_u[lo - d:hi - d, :]
        s_u[lo:hi, :] = a0 * u1 + u0
        s_a[lo:hi, :] = a0 * a1
        d *= 2
    h = s_u[lo:hi, :] + s_a[lo:hi, :] * hcar[...]
    hcar[...] = h[TB - 1:TB, :]
    gcube = gate * gate * gate
    gelu = 0.5 * gate * (1.0 + jnp.tanh(math.sqrt(2.0 / math.pi) * (gate + 0.044715 * gcube)))
    ylru_ref[0] = h * gelu


def _mlstm_kernel(p_ref, ifb_ref, gng_ref, gnb_ref, y_ref, c_st, n_st, m_st):
    t = pl.program_id(1)

    @pl.when(t == 0)
    def _():
        c_st[...] = jnp.zeros((GROUP_W, GROUP_W), F32)
        n_st[...] = jnp.zeros((1, GROUP_W), F32)
        m_st[...] = jnp.zeros((1, LANE), F32)

    _, valid = _valid_rows(t)
    hms = _head_masks()
    q = jnp.where(valid, p_ref[0, :, 0:256], 0.0) * (HEAD ** -0.5)
    k = jnp.where(valid, p_ref[0, :, 256:512], 0.0)
    v = jnp.where(valid, p_ref[0, :, 512:768], 0.0)
    o = p_ref[0, :, 768:1024]
    g = p_ref[0, :, 1024:ML_COLS_PAD] + ifb_ref[...]
    lane = lax.broadcasted_iota(jnp.int32, (1, LANE), 1)
    is_i = lane < N_H
    is_f = jnp.logical_and(lane >= N_H, lane < 2 * N_H)
    logf = jnp.where(jnp.logical_and(is_f, valid), -_softplus(-g), 0.0)
    logi = jnp.where(is_i, jnp.where(valid, g, NEG), 0.0)

    tcol = lax.broadcasted_iota(jnp.int32, (TB, 1), 0)
    srow = lax.broadcasted_iota(jnp.int32, (1, TB), 1)
    causal = srow <= tcol
    tri = jnp.where(causal, 1.0, 0.0).astype(BF16)
    bcum = _dot_exact_lhs(tri, logf)
    srow8 = lax.broadcasted_iota(jnp.int32, (SUBLANE, LANE), 0)
    lane8 = lax.broadcasted_iota(jnp.int32, (SUBLANE, LANE), 1)
    psel = (jnp.where(lane8 == srow8, 1.0, 0.0) - jnp.where(lane8 == srow8 + N_H, 1.0, 0.0)).astype(BF16)
    zt = _dot_exact_lhs(psel, logi + bcum, _NT)

    m_prev = m_st[...]
    m_inter_t = bcum + m_prev
    kb, vb = k.astype(BF16), v.astype(BF16)
    qc = _bdot(q, c_st[...])
    qn = q * n_st[...]

    num = jnp.zeros((TB, GROUP_W), F32)
    wk_e = jnp.zeros((TB, GROUP_W), F32)
    sc_row = jnp.zeros((1, GROUP_W), F32)
    m_new_t = jnp.zeros((1, LANE), F32)
    for h in range(N_H):
        hm = hms[h]
        b_h = _col(bcum, N_H + h)
        dmat = jnp.where(causal, b_h + zt[h:h + 1, :], NEG)
        m_intra = jnp.max(dmat, axis=1, keepdims=True)
        m_inter = _col(m_inter_t, N_H + h)
        m_t = jnp.maximum(m_intra, m_inter)
        sc = _dot(jnp.where(hm, q, 0.0).astype(BF16), kb, _NT)
        pm = sc * jnp.exp(dmat - m_t)
        num_h = _dot(pm.astype(BF16), jnp.where(hm, v, 0.0).astype(BF16))
        w_int = jnp.exp(m_inter - m_t)
        den = (jnp.sum(pm, axis=1, keepdims=True)
               + w_int * jnp.sum(jnp.where(hm, qn, 0.0), axis=1, keepdims=True))
        denom = jnp.maximum(jnp.abs(den), jnp.exp(-m_t))
        num = num + (num_h + w_int * jnp.where(hm, qc, 0.0)) / denom
        b_last = b_h[TB - 1:TB, :]
        g_loc = b_last + (_col(logi, h) - b_h)
        m_loc = jnp.max(g_loc, axis=0, keepdims=True)
        m_old = jnp.sum(jnp.where(lane == N_H + h, m_prev, 0.0), axis=1, keepdims=True)
        m_new = jnp.maximum(b_last + m_old, m_loc)
        wk_e = wk_e + jnp.where(hm, jnp.exp(g_loc - m_new), 0.0)
        sc_row = sc_row + jnp.where(hm, jnp.exp(b_last + m_old - m_new), 0.0)
        m_new_t = m_new_t + jnp.where(lane == N_H + h, m_new, 0.0)

    kw = k * wk_e
    rr = lax.broadcasted_iota(jnp.int32, (GROUP_W, 1), 0) // HEAD
    cc = lax.broadcasted_iota(jnp.int32, (1, GROUP_W), 1) // HEAD
    c_st[...] = sc_row * c_st[...] + jnp.where(rr == cc, _dot(kw.astype(BF16), vb, _TN), 0.0)
    n_st[...] = sc_row * n_st[...] + jnp.sum(kw, axis=0, keepdims=True)
    m_st[...] = m_new_t

    y_ref[0] = _head_norm(num, gng_ref[...], gnb_ref[...], hms) * jax.nn.sigmoid(o)


def _rwkv_kernel(has_vfirst, *refs):
    if has_vfirst:
        (p_ref, vf_ref, mu_ref, w0_ref, wup_ref, a0_ref, aup_ref, gup_ref, kk_ref, ka_ref, rk_ref, gng_ref,
         gnb_ref, v0_ref, vdn_ref, vup_ref, y_ref, pext, s_st) = refs
    else:
        (p_ref, mu_ref, w0_ref, wup_ref, a0_ref, aup_ref, gup_ref, kk_ref, ka_ref, rk_ref, gng_ref,
         gnb_ref, y_ref, vfo_ref, pext, s_st) = refs
    t = pl.program_id(1)

    @pl.when(t == 0)
    def _():
        pext[0:SUBLANE, :] = jnp.zeros((SUBLANE, RW_COLS), F32)
        s_st[...] = jnp.zeros((GROUP_W, GROUP_W), F32)

    _, valid = _valid_rows(t)
    hms = _head_masks()
    p = jnp.where(valid, p_ref[0], 0.0)
    pext[SUBLANE:SUBLANE + TB, :] = p
    prev = pext[SUBLANE - 1:SUBLANE - 1 + TB, :]
    pext[0:SUBLANE, :] = pext[TB:TB + SUBLANE, :]
    pm = p + (prev - p) * mu_ref[...]
    r = pm[:, 0:256]
    k = pm[:, 256:512]
    v = pm[:, 512:768]
    lora = pm[:, 768:RW_COLS]
    w = -_softplus(-(w0_ref[...] + _bdot(jnp.tanh(lora), wup_ref[...]))) - 0.5
    ld = -jnp.exp(w)
    a = jax.nn.sigmoid(a0_ref[...] + _bdot(lora, aup_ref[...]))
    g = _bdot(jax.nn.sigmoid(lora), gup_ref[...])
    if has_vfirst:
        vmix = jax.nn.sigmoid(v0_ref[...] + _bdot(_bdot(v, vdn_ref[...]), vup_ref[...]))
        v = v + (vf_ref[0] - v) * vmix
    else:
        vfo_ref[0] = v
    kk = k * kk_ref[...]
    kk = kk * lax.rsqrt(_head_sum(kk * kk, hms) + 1e-12)
    k = k * (1.0 + (a - 1.0) * ka_ref[...])
    bvec = kk * a

    tcol = lax.broadcasted_iota(jnp.int32, (TB, 1), 0)
    srow = lax.broadcasted_iota(jnp.int32, (1, TB), 1)
    same = (tcol // RW_CHUNK) == (srow // RW_CHUNK)
    incl = jnp.logical_and(same, srow <= tcol)
    strict = jnp.logical_and(same, srow < tcol)
    eye = srow == tcol
    lc = _dot_exact_lhs(jnp.where(incl, 1.0, 0.0).astype(BF16), ld)
    n_chunks = TB // RW_CHUNK
    lc_last = [lc[(c + 1) * RW_CHUNK - 1:(c + 1) * RW_CHUNK, :] for c in range(n_chunks)]
    lc_end = jnp.concatenate([jnp.broadcast_to(x, (RW_CHUNK, GROUP_W)) for x in lc_last], axis=0)
    e_neg = jnp.exp(-lc)
    kap = kk * jnp.exp(lc - ld)
    kt = (k * e_neg).astype(BF16)
    bt = (bvec * e_neg).astype(BF16)
    rt = r * jnp.exp(lc)
    e_end = jnp.exp(lc_end - lc)
    kbar = k * e_end
    bbar = bvec * e_end

    p1_all = jnp.zeros((TB, GROUP_W), F32)
    p2_all = jnp.zeros((TB, GROUP_W), F32)
    y0 = jnp.zeros((TB, GROUP_W), F32)
    qq = jnp.zeros((TB, GROUP_W), F32)
    for h in range(N_H):
        hm = hms[h]
        kap_h = jnp.where(hm, kap, 0.0).astype(BF16)
        rt_h = jnp.where(hm, rt, 0.0)
        rt_hb = rt_h.astype(BF16)
        v_h = jnp.where(hm, v, 0.0).astype(BF16)
        a_kb = _dot(kap_h, bt, _NT)
        a_kk = _dot(kap_h, kt, _NT)
        a_rb = jnp.where(incl, _dot(rt_hb, bt, _NT), 0.0).astype(BF16)
        a_rk = jnp.where(incl, _dot(rt_hb, kt, _NT), 0.0).astype(BF16)
        nmat = jnp.where(strict, -a_kb, 0.0)
        tinv = jnp.where(eye, 1.0, 0.0) + nmat
        npow = nmat
        for _ in range(int(math.log2(RW_CHUNK)) - 1):
            npow = _bdot(npow, npow)
            tinv = tinv + _bdot(tinv, npow)
        tinv_b = tinv.astype(BF16)
        akv = _dot(jnp.where(strict, a_kk, 0.0).astype(BF16), v_h)
        p1 = _dot(tinv_b, kap_h)
        p2 = _dot(tinv_b, akv.astype(BF16))
        y0 = y0 + _dot(a_rk, v_h) - _dot(a_rb, p2.astype(BF16))
        qq = qq + rt_h - _dot(a_rb, p1.astype(BF16))
        p1_all = p1_all + p1
        p2_all = p2_all + p2

    rr = lax.broadcasted_iota(jnp.int32, (GROUP_W, 1), 0) // HEAD
    cc = lax.broadcasted_iota(jnp.int32, (1, GROUP_W), 1) // HEAD
    blockdiag = rr == cc
    s_val = s_st[...]
    ys = []
    for c in range(n_chunks):
        sl = slice(c * RW_CHUNK, (c + 1) * RW_CHUNK)
        lhs = jnp.concatenate([p1_all[sl], qq[sl]], axis=0)
        res = _bdot(lhs, s_val, _NT)
        u_c = res[0:RW_CHUNK] + p2_all[sl]
        ys.append(y0[sl] + res[RW_CHUNK:2 * RW_CHUNK])
        lt = jnp.concatenate([v[sl], -u_c], axis=0)
        rt2 = jnp.concatenate([kbar[sl], bbar[sl]], axis=0)
        s_val = s_val * jnp.exp(lc_last[c]) + jnp.where(blockdiag, _bdot(lt, rt2, _TN), 0.0)
    s_st[...] = s_val
    y = jnp.concatenate(ys, axis=0)

    yn = _head_norm(y, gng_ref[...], gnb_ref[...], hms)
    bonus = _head_sum(r * k * rk_ref[...], hms) * v
    y_ref[0] = (yn + bonus) * g


def _tspec(width):
    return pl.BlockSpec((1, TB, width), lambda b, t: (b, t, 0))


def _cspec(shape):
    return pl.BlockSpec(shape, lambda b, t: (0,) * len(shape))


def _mixer_call(kernel, name, batch, tp, time_ins, const_ins, out_widths, scratch):
    grid = (batch, tp // TB)
    in_specs = [_tspec(a.shape[2]) for a in time_ins] + [_cspec(a.shape) for a in const_ins]
    out_shape = [jax.ShapeDtypeStruct((batch, tp, w), F32) for w in out_widths]
    out_specs = [_tspec(w) for w in out_widths]
    single = len(out_widths) == 1
    return pl.pallas_call(
        kernel,
        grid=grid,
        in_specs=in_specs,
        out_specs=out_specs[0] if single else out_specs,
        out_shape=out_shape[0] if single else out_shape,
        scratch_shapes=scratch,
        compiler_params=pltpu.CompilerParams(dimension_semantics=("arbitrary", "arbitrary"),
                                             vmem_limit_bytes=VMEM_LIMIT),
        name=name,
    )(*time_ins, *const_ins)


def _block_diag(w):
    eye = jnp.eye(N_H, dtype=w.dtype)
    return jnp.einsum('gcd,gh->gchd', w, eye).reshape(GROUP_W, GROUP_W)


def _pad_rows(w, total, offset):
    return jnp.zeros((total, w.shape[1]), w.dtype).at[offset:offset + w.shape[0]].set(w)


def _row(v):
    return v.reshape(1, -1)


@jax.jit
def kernel(x, meta, emb_ln_g, emb_ln_b, w_in, w_out, pool_w, pool_scale, rw_mu, rw_w0, rw_w_up, rw_a0, rw_a_up, rw_g_up, rw_k_k, rw_k_a, rw_r_k, rw_gn_g, rw_gn_b, rw_v0, rw_v_down, rw_v_up, lru_conv_w, lru_conv_b, lru_ga_w, lru_ga_b, lru_gx_w, lru_gx_b, lru_lambda, ml_if_b, ml_gn_g, ml_gn_b, ln1_g, ln1_b, ln2_g, ln2_b, mlp_w1, mlp_w2):
    batch, seq, _ = x.shape
    depth = w_in.shape[0]
    assert seq % TB == 0 and x.shape[2] == D_MODEL
    tp = TB + seq
    rows = batch * tp
    alpha = (2 * depth) ** 0.25

    front = jnp.zeros((batch, TB - N_META, D_MODEL), x.dtype)
    metab = jnp.broadcast_to(meta[None].astype(x.dtype), (batch, N_META, D_MODEL))
    hcat = jnp.concatenate([front, metab, x], axis=1).reshape(rows, D_MODEL)
    h = _rows_call(_embed_ln_kernel, "embed_ln", rows, [hcat], [_row(emb_ln_g), _row(emb_ln_b)], [D_MODEL])

    w_in_b = jnp.pad(w_in, ((0, 0), (0, 0), (0, D_IN_PAD - w_in.shape[2]))).astype(BF16)
    w_out_b = w_out.astype(BF16)
    w1_b = mlp_w1.astype(BF16)
    w2_b = mlp_w2.astype(BF16)

    v_first = None
    for l in range(depth):
        p_pool, p_rw, p_lru, p_ml = _rows_call(
            _inproj_kernel, "in_proj", rows, [h], [w_in_b[l]], [GROUP_W, RW_COLS, 2 * GROUP_W, ML_COLS_PAD])
        p_pool = p_pool.reshape(batch, tp, GROUP_W)
        p_rw = p_rw.reshape(batch, tp, RW_COLS)
        p_lru = p_lru.reshape(batch, tp, 2 * GROUP_W)
        p_ml = p_ml.reshape(batch, tp, ML_COLS_PAD)

        wax = jnp.concatenate([_block_diag(lru_ga_w[l]), _block_diag(lru_gx_w[l])], axis=1).astype(BF16)
        bax = jnp.concatenate([lru_ga_b[l], lru_gx_b[l]]).reshape(1, 2 * GROUP_W)
        y_pool, y_lru = _mixer_call(
            _pool_lru_kernel, "pool_lru", batch, tp, [p_pool, p_lru],
            [_block_diag(pool_w[l]).astype(BF16), _row(pool_scale[l]), lru_conv_w[l], _row(lru_conv_b[l]),
             wax, bax, _row(lru_lambda[l])],
            [GROUP_W, GROUP_W],
            [pltpu.VMEM((TB + POOL_MAX_W, GROUP_W), F32), pltpu.VMEM((TB + SUBLANE, GROUP_W), F32),
             pltpu.VMEM((TB + _SCAN_PAD, GROUP_W), F32), pltpu.VMEM((TB + _SCAN_PAD, GROUP_W), F32),
             pltpu.VMEM((1, GROUP_W), F32)])

        ifb = jnp.zeros((1, LANE), F32).at[0, 0:2 * N_H].set(ml_if_b[l])
        y_ml = _mixer_call(
            _mlstm_kernel, "mlstm", batch, tp, [p_ml], [ifb, _row(ml_gn_g[l]), _row(ml_gn_b[l])], [GROUP_W],
            [pltpu.VMEM((GROUP_W, GROUP_W), F32), pltpu.VMEM((1, GROUP_W), F32), pltpu.VMEM((1, LANE), F32)])

        wup = _pad_rows(rw_w_up[l], LANE, 0).astype(BF16)
        aup = _pad_rows(rw_a_up[l], LANE, 32).astype(BF16)
        gup = _pad_rows(rw_g_up[l], LANE, 64).astype(BF16)
        rw_consts = [_row(rw_mu[l]), _row(rw_w0[l]), wup, _row(rw_a0[l]), aup, gup, _row(rw_k_k[l]),
                     _row(rw_k_a[l]), _row(rw_r_k[l]), _row(rw_gn_g[l]), _row(rw_gn_b[l])]
        rw_scratch = [pltpu.VMEM((TB + SUBLANE, RW_COLS), F32), pltpu.VMEM((GROUP_W, GROUP_W), F32)]
        if l == 0:
            y_rw, v_first = _mixer_call(
                functools.partial(_rwkv_kernel, False), "rwkv7_first", batch, tp, [p_rw], rw_consts,
                [GROUP_W, GROUP_W], rw_scratch)
        else:
            vdn = jnp.pad(rw_v_down[l - 1], ((0, 0), (0, LANE - rw_v_down.shape[2]))).astype(BF16)
            vup = _pad_rows(rw_v_up[l - 1], LANE, 0).astype(BF16)
            y_rw = _mixer_call(
                functools.partial(_rwkv_kernel, True), "rwkv7", batch, tp, [p_rw, v_first],
                rw_consts + [_row(rw_v0[l - 1]), vdn, vup], [GROUP_W], rw_scratch)

        ys = [y.reshape(rows, GROUP_W) for y in (y_pool, y_rw, y_lru, y_ml)]
        h = _rows_call(functools.partial(_outproj_kernel, alpha), "out_proj", rows, [h] + ys,
                       [w_out_b[l], _row(ln1_g[l]), _row(ln1_b[l])], [D_MODEL])
        h = _rows_call(functools.partial(_mlp_kernel, alpha), "mlp", rows, [h],
                       [w1_b[l], w2_b[l], _row(ln2_g[l]), _row(ln2_b[l])], [D_MODEL])

    return h.reshape(batch, tp, D_MODEL)[:, TB:, :]
```

```python
import functools
import math

import jax
import jax.numpy as jnp
from jax import lax
from jax.experimental import pallas as pl
from jax.experimental.pallas import tpu as pltpu

D_MODEL = 1024
N_META = 16
GROUP_W = 256
HEAD = 64
N_H = 4
POOL_MAX_W = 16
CONV_W = 4
LRU_C = 8.0
LN_EPS = 1e-5
GN_EPS = 64e-5
NEG = -1e30
D_FF = 4096
RW_COLS = 896
ML_COLS_PAD = 1152
D_IN_PAD = 2816
LANE = 128
SUBLANE = 8

TB = 256
RW_CHUNK = 64
TM = 512
VMEM_LIMIT = 56 * 1024 * 1024

F32 = jnp.float32
BF16 = jnp.bfloat16

_NN = (((1,), (0,)), ((), ()))
_NT = (((1,), (1,)), ((), ()))
_TN = (((0,), (0,)), ((), ()))


def _dot(a, b, dims=_NN):
    return lax.dot_general(a, b, dims, preferred_element_type=F32)


def _bdot(a, b, dims=_NN):
    return _dot(a.astype(BF16), b.astype(BF16), dims)


def _split3(x):
    hi = x.astype(BF16)
    r1 = x - hi.astype(F32)
    mid = r1.astype(BF16)
    lo = (r1 - mid.astype(F32)).astype(BF16)
    return hi, mid, lo


def _dot_exact_lhs(e_bf16, x, dims=_NN):
    hi, mid, lo = _split3(x)
    return _dot(e_bf16, hi, dims) + _dot(e_bf16, mid, dims) + _dot(e_bf16, lo, dims)


def _softplus(x):
    return jnp.maximum(x, 0.0) + jnp.log1p(jnp.exp(-jnp.abs(x)))


def _layer_norm(x, g, b):
    mu = jnp.mean(x, axis=-1, keepdims=True)
    xc = x - mu
    var = jnp.mean(xc * xc, axis=-1, keepdims=True)
    return xc * lax.rsqrt(var + LN_EPS) * g + b


def _head_masks(width=GROUP_W):
    lane = lax.broadcasted_iota(jnp.int32, (1, width), 1)
    return [(lane // HEAD) == h for h in range(N_H)]


def _head_sum(x, hms):
    out = jnp.zeros_like(x)
    for hm in hms:
        s = jnp.sum(jnp.where(hm, x, 0.0), axis=1, keepdims=True)
        out = out + jnp.where(hm, s, 0.0)
    return out


def _head_norm(y, g, b, hms):
    mu = _head_sum(y, hms) * (1.0 / HEAD)
    yc = y - mu
    var = _head_sum(yc * yc, hms) * (1.0 / HEAD)
    return yc * lax.rsqrt(var + GN_EPS) * g + b


def _col(x, j):
    lane = lax.broadcasted_iota(jnp.int32, (1, x.shape[1]), 1)
    return jnp.sum(jnp.where(lane == j, x, 0.0), axis=1, keepdims=True)


def _valid_rows(t):
    row = t * TB + lax.broadcasted_iota(jnp.int32, (TB, 1), 0)
    return row, row >= (TB - N_META)


def _embed_ln_kernel(x_ref, g_ref, b_ref, o_ref):
    o_ref[...] = _layer_norm(x_ref[...], g_ref[...], b_ref[...])


def _inproj_kernel(h_ref, w_ref, o_pool, o_rw, o_lru, o_ml):
    hb = h_ref[...].astype(BF16)
    o_pool[...] = _dot(hb, w_ref[:, 0:256])
    o_rw[...] = _dot(hb, w_ref[:, 256:1152])
    o_lru[...] = _dot(hb, w_ref[:, 1152:1664])
    o_ml[...] = _dot(hb, w_ref[:, 1664:D_IN_PAD])


def _outproj_kernel(alpha, h_ref, y0, y1, y2, y3, w_ref, g_ref, b_ref, o_ref):
    mix = _dot(y0[...].astype(BF16), w_ref[0:256, :])
    mix += _dot(y1[...].astype(BF16), w_ref[256:512, :])
    mix += _dot(y2[...].astype(BF16), w_ref[512:768, :])
    mix += _dot(y3[...].astype(BF16), w_ref[768:1024, :])
    o_ref[...] = _layer_norm(alpha * h_ref[...] + mix, g_ref[...], b_ref[...])


def _mlp_kernel(alpha, h_ref, w1_ref, w2_ref, g_ref, b_ref, o_ref):
    h = h_ref[...]
    hb = h.astype(BF16)
    acc = jnp.zeros_like(h)
    for j in range(D_FF // D_MODEL):
        cs = slice(j * D_MODEL, (j + 1) * D_MODEL)
        z = jnp.maximum(_dot(hb, w1_ref[:, cs]), 0.0)
        acc += _dot((z * z).astype(BF16), w2_ref[cs, :])
    o_ref[...] = _layer_norm(alpha * h + acc, g_ref[...], b_ref[...])


def _row_spec(width):
    return pl.BlockSpec((TM, width), lambda i: (i, 0))


def _const_spec(shape):
    return pl.BlockSpec(shape, lambda i: (0,) * len(shape))


def _rows_call(kernel, name, rows, row_ins, const_ins, out_widths):
    n = pl.cdiv(rows, TM)
    in_specs = [_row_spec(a.shape[1]) for a in row_ins] + [_const_spec(a.shape) for a in const_ins]
    out_shape = [jax.ShapeDtypeStruct((rows, w), F32) for w in out_widths]
    out_specs = [_row_spec(w) for w in out_widths]
    single = len(out_widths) == 1
    res = pl.pallas_call(
        kernel,
        grid=(n,),
        in_specs=in_specs,
        out_specs=out_specs[0] if single else out_specs,
        out_shape=out_shape[0] if single else out_shape,
        compiler_params=pltpu.CompilerParams(dimension_semantics=("arbitrary",), vmem_limit_bytes=VMEM_LIMIT),
        name=name,
    )(*row_ins, *const_ins)
    return res


_SCAN_PAD = TB // 2
_INIT, _BODY = "init", "body"
_DONE = object()
_LEAD_STAGES = 5


def _pool_lru_kernel(phase, pp_ref, plru_ref, poolw_ref, pscale_ref, convw_ref, convb_ref, wax_ref, bax_ref,
                     lam_ref, ypool_ref, ylru_ref, pext, cext, s_a, s_u, hcar):
    t = pl.program_id(1)
    if phase == _INIT:
        @pl.when(t == 0)
        def _():
            pext[0:POOL_MAX_W, :] = jnp.zeros((POOL_MAX_W, GROUP_W), F32)
            cext[0:SUBLANE, :] = jnp.zeros((SUBLANE, GROUP_W), F32)
            hcar[...] = jnp.zeros((1, GROUP_W), F32)
            s_a[0:_SCAN_PAD, :] = jnp.ones((_SCAN_PAD, GROUP_W), F32)
            s_u[0:_SCAN_PAD, :] = jnp.zeros((_SCAN_PAD, GROUP_W), F32)
        return

    row, valid = _valid_rows(t)

    u = jnp.where(valid, pp_ref[0], 0.0)
    pext[POOL_MAX_W:POOL_MAX_W + TB, :] = u
    lane = lax.broadcasted_iota(jnp.int32, (1, GROUP_W), 1)
    win = jnp.left_shift(2, lane // HEAD)
    acc = jnp.zeros((TB, GROUP_W), F32)
    for j in range(POOL_MAX_W):
        acc = acc + jnp.where(win > j, pext[POOL_MAX_W - j:POOL_MAX_W - j + TB, :], 0.0)
        if j % 4 == 3:
            yield
    pos1 = (row - (TB - N_META) + 1).astype(F32)
    cnt = jnp.clip(pos1, 1.0, win.astype(F32))
    dlt = acc / cnt - u
    ypool_ref[0] = _bdot(dlt, poolw_ref[...]) * pscale_ref[...]
    pext[0:POOL_MAX_W, :] = pext[TB:TB + POOL_MAX_W, :]
    yield

    x = jnp.where(valid, plru_ref[0, :, 0:GROUP_W], 0.0)
    gate = plru_ref[0, :, GROUP_W:2 * GROUP_W]
    cext[SUBLANE:SUBLANE + TB, :] = x
    xc = jnp.zeros((TB, GROUP_W), F32) + convb_ref[...]
    for j in range(CONV_W):
        off = SUBLANE - (CONV_W - 1) + j
        xc = xc + convw_ref[j:j + 1, :] * cext[off:off + TB, :]
    cext[0:SUBLANE, :] = cext[TB:TB + SUBLANE, :]
    yield
    z = _bdot(xc, wax_ref[...]) + bax_ref[...]
    r = jax.nn.sigmoid(z[:, 0:GROUP_W])
    i = jax.nn.sigmoid(z[:, GROUP_W:2 * GROUP_W])
    yield
    log_a = (-LRU_C) * r * _softplus(-lam_ref[...])
    a = jnp.exp(log_a)
    uu = jnp.where(valid, jnp.sqrt(-jnp.tanh(log_a) * (a * a + 1.0)) * (i * xc), 0.0)
    lo, hi = _SCAN_PAD, _SCAN_PAD + TB
    s_a[lo:hi, :] = a
    s_u[lo:hi, :] = uu
    yield
    d = 1
    while d < TB:
        a0, u0 = s_a[lo:hi, :], s_u[lo:hi, :]
        a1, u1 = s_a[lo - d:hi - d, :], s_u[lo - d:hi - d, :]
        s_u[lo:hi, :] = a0 * u1 + u0
        s_a[lo:hi, :] = a0 * a1
        d *= 2
        yield
    h = s_u[lo:hi, :] + s_a[lo:hi, :] * hcar[...]
    hcar[...] = h[TB - 1:TB, :]
    yield
    gcube = gate * gate * gate
    gelu = 0.5 * gate * (1.0 + jnp.tanh(math.sqrt(2.0 / math.pi) * (gate + 0.044715 * gcube)))
    ylru_ref[0] = h * gelu


def _mlstm_kernel(phase, p_ref, ifb_ref, gng_ref, gnb_ref, y_ref, c_st, n_st, m_st):
    t = pl.program_id(1)
    if phase == _INIT:
        @pl.when(t == 0)
        def _():
            c_st[...] = jnp.zeros((GROUP_W, GROUP_W), F32)
            n_st[...] = jnp.zeros((1, GROUP_W), F32)
            m_st[...] = jnp.zeros((1, LANE), F32)
        return

    _, valid = _valid_rows(t)
    hms = _head_masks()
    q = jnp.where(valid, p_ref[0, :, 0:256], 0.0) * (HEAD ** -0.5)
    k = jnp.where(valid, p_ref[0, :, 256:512], 0.0)
    v = jnp.where(valid, p_ref[0, :, 512:768], 0.0)
    o = p_ref[0, :, 768:1024]
    g = p_ref[0, :, 1024:ML_COLS_PAD] + ifb_ref[...]
    lane = lax.broadcasted_iota(jnp.int32, (1, LANE), 1)
    is_i = lane < N_H
    is_f = jnp.logical_and(lane >= N_H, lane < 2 * N_H)
    logf = jnp.where(jnp.logical_and(is_f, valid), -_softplus(-g), 0.0)
    logi = jnp.where(is_i, jnp.where(valid, g, NEG), 0.0)

    tcol = lax.broadcasted_iota(jnp.int32, (TB, 1), 0)
    srow = lax.broadcasted_iota(jnp.int32, (1, TB), 1)
    causal = srow <= tcol
    tri = jnp.where(causal, 1.0, 0.0).astype(BF16)
    bcum = _dot_exact_lhs(tri, logf)
    srow8 = lax.broadcasted_iota(jnp.int32, (SUBLANE, LANE), 0)
    lane8 = lax.broadcasted_iota(jnp.int32, (SUBLANE, LANE), 1)
    psel = (jnp.where(lane8 == srow8, 1.0, 0.0) - jnp.where(lane8 == srow8 + N_H, 1.0, 0.0)).astype(BF16)
    yield
    zt = _dot_exact_lhs(psel, logi + bcum, _NT)
    yield

    m_prev = m_st[...]
    m_inter_t = bcum + m_prev
    kb, vb = k.astype(BF16), v.astype(BF16)
    qc = _bdot(q, c_st[...])
    qn = q * n_st[...]

    num = jnp.zeros((TB, GROUP_W), F32)
    wk_e = jnp.zeros((TB, GROUP_W), F32)
    sc_row = jnp.zeros((1, GROUP_W), F32)
    m_new_t = jnp.zeros((1, LANE), F32)
    for h in range(N_H):
        hm = hms[h]
        b_h = _col(bcum, N_H + h)
        dmat = jnp.where(causal, b_h + zt[h:h + 1, :], NEG)
        m_intra = jnp.max(dmat, axis=1, keepdims=True)
        m_inter = _col(m_inter_t, N_H + h)
        m_t = jnp.maximum(m_intra, m_inter)
        sc = _dot(jnp.where(hm, q, 0.0).astype(BF16), kb, _NT)
        pm = sc * jnp.exp(dmat - m_t)
        yield
        num_h = _dot(pm.astype(BF16), jnp.where(hm, v, 0.0).astype(BF16))
        w_int = jnp.exp(m_inter - m_t)
        den = (jnp.sum(pm, axis=1, keepdims=True)
               + w_int * jnp.sum(jnp.where(hm, qn, 0.0), axis=1, keepdims=True))
        denom = jnp.maximum(jnp.abs(den), jnp.exp(-m_t))
        num = num + (num_h + w_int * jnp.where(hm, qc, 0.0)) / denom
        yield
        b_last = b_h[TB - 1:TB, :]
        g_loc = b_last + (_col(logi, h) - b_h)
        m_loc = jnp.max(g_loc, axis=0, keepdims=True)
        m_old = jnp.sum(jnp.where(lane == N_H + h, m_prev, 0.0), axis=1, keepdims=True)
        m_new = jnp.maximum(b_last + m_old, m_loc)
        wk_e = wk_e + jnp.where(hm, jnp.exp(g_loc - m_new), 0.0)
        sc_row = sc_row + jnp.where(hm, jnp.exp(b_last + m_old - m_new), 0.0)
        m_new_t = m_new_t + jnp.where(lane == N_H + h, m_new, 0.0)
        yield

    kw = k * wk_e
    rr = lax.broadcasted_iota(jnp.int32, (GROUP_W, 1), 0) // HEAD
    cc = lax.broadcasted_iota(jnp.int32, (1, GROUP_W), 1) // HEAD
    c_st[...] = sc_row * c_st[...] + jnp.where(rr == cc, _dot(kw.astype(BF16), vb, _TN), 0.0)
    n_st[...] = sc_row * n_st[...] + jnp.sum(kw, axis=0, keepdims=True)
    m_st[...] = m_new_t
    yield

    y_ref[0] = _head_norm(num, gng_ref[...], gnb_ref[...], hms) * jax.nn.sigmoid(o)


def _rwkv_kernel(has_vfirst, phase, *refs):
    if has_vfirst:
        (p_ref, vf_ref, mu_ref, w0_ref, wup_ref, a0_ref, aup_ref, gup_ref, kk_ref, ka_ref, rk_ref, gng_ref,
         gnb_ref, v0_ref, vdn_ref, vup_ref, y_ref, pext, s_st) = refs
    else:
        (p_ref, mu_ref, w0_ref, wup_ref, a0_ref, aup_ref, gup_ref, kk_ref, ka_ref, rk_ref, gng_ref,
         gnb_ref, y_ref, vfo_ref, pext, s_st) = refs
    t = pl.program_id(1)
    if phase == _INIT:
        @pl.when(t == 0)
        def _():
            pext[0:SUBLANE, :] = jnp.zeros((SUBLANE, RW_COLS), F32)
            s_st[...] = jnp.zeros((GROUP_W, GROUP_W), F32)
        return

    _, valid = _valid_rows(t)
    hms = _head_masks()
    p = jnp.where(valid, p_ref[0], 0.0)
    pext[SUBLANE:SUBLANE + TB, :] = p
    prev = pext[SUBLANE - 1:SUBLANE - 1 + TB, :]
    pext[0:SUBLANE, :] = pext[TB:TB + SUBLANE, :]
    pm = p + (prev - p) * mu_ref[...]
    yield
    r = pm[:, 0:256]
    k = pm[:, 256:512]
    v = pm[:, 512:768]
    lora = pm[:, 768:RW_COLS]
    w = -_softplus(-(w0_ref[...] + _bdot(jnp.tanh(lora), wup_ref[...]))) - 0.5
    ld = -jnp.exp(w)
    a = jax.nn.sigmoid(a0_ref[...] + _bdot(lora, aup_ref[...]))
    g = _bdot(jax.nn.sigmoid(lora), gup_ref[...])
    yield
    if has_vfirst:
        vmix = jax.nn.sigmoid(v0_ref[...] + _bdot(_bdot(v, vdn_ref[...]), vup_ref[...]))
        v = v + (vf_ref[0] - v) * vmix
    else:
        vfo_ref[0] = v
    kk = k * kk_ref[...]
    kk = kk * lax.rsqrt(_head_sum(kk * kk, hms) + 1e-12)
    k = k * (1.0 + (a - 1.0) * ka_ref[...])
    bvec = kk * a
    yield

    tcol = lax.broadcasted_iota(jnp.int32, (TB, 1), 0)
    srow = lax.broadcasted_iota(jnp.int32, (1, TB), 1)
    cum_mask = jnp.logical_and((tcol // RW_CHUNK) == (srow // RW_CHUNK), srow <= tcol)
    lc = _dot_exact_lhs(jnp.where(cum_mask, 1.0, 0.0).astype(BF16), ld)
    n_chunks = TB // RW_CHUNK
    chunks = range(n_chunks)
    lc_last = [lc[(c + 1) * RW_CHUNK - 1:(c + 1) * RW_CHUNK, :] for c in chunks]
    lc_end = jnp.concatenate([jnp.broadcast_to(x, (RW_CHUNK, GROUP_W)) for x in lc_last], axis=0)
    yield
    e_neg = jnp.exp(-lc)
    kap = (kk * jnp.exp(lc - ld)).astype(BF16)
    kt = (k * e_neg).astype(BF16)
    bt = (bvec * e_neg).astype(BF16)
    rt = r * jnp.exp(lc)
    rtb = rt.astype(BF16)
    vb = v.astype(BF16)
    e_end = jnp.exp(lc_end - lc)
    kbar = (k * e_end).astype(BF16)
    bbar = (bvec * e_end).astype(BF16)
    yield

    rr = lax.broadcasted_iota(jnp.int32, (GROUP_W, 1), 0) // HEAD
    cc = lax.broadcasted_iota(jnp.int32, (1, GROUP_W), 1) // HEAD
    blockdiag = rr == cc
    bd_b = jnp.where(blockdiag, 1.0, 0.0).astype(BF16)

    def bd(xb):
        return jnp.concatenate([xb] * N_H, axis=0) * bd_b

    tloc = lax.broadcasted_iota(jnp.int32, (RW_CHUNK, 1), 0)
    sloc = lax.broadcasted_iota(jnp.int32, (1, GROUP_W), 1) % RW_CHUNK
    strict = sloc < tloc
    incl = sloc <= tloc
    eye_pk = jnp.where(sloc == tloc, 1.0, 0.0)
    sls = [slice(c * RW_CHUNK, (c + 1) * RW_CHUNK) for c in chunks]

    lhs_kr = [jnp.concatenate([kap[sl], rtb[sl]], axis=0) for sl in sls]
    s_k = [_dot(lhs_kr[c], bd(kt[sls[c]]), _NT) for c in chunks]
    s_b = [_dot(lhs_kr[c], bd(bt[sls[c]]), _NT) for c in chunks]
    yield
    nmat = [jnp.where(strict, -s_b[c][0:RW_CHUNK], 0.0) for c in chunks]
    zs = [eye_pk + nmat[c] for c in chunks]
    ypow = [nmat[c].astype(BF16) for c in chunks]
    ypow = [_dot(ypow[c], bd(ypow[c])).astype(BF16) for c in chunks]
    n_iter = int(math.log2(RW_CHUNK)) - 1
    yield
    for it in range(n_iter):
        rhs = [bd(ypow[c]) for c in chunks]
        if it + 1 < n_iter:
            res = [_dot(jnp.concatenate([ypow[c], zs[c].astype(BF16)], axis=0), rhs[c]) for c in chunks]
            ypow = [res[c][0:RW_CHUNK].astype(BF16) for c in chunks]
            zs = [zs[c] + res[c][RW_CHUNK:2 * RW_CHUNK] for c in chunks]
        else:
            zs = [zs[c] + _dot(zs[c].astype(BF16), rhs[c]) for c in chunks]
        yield
    tinv = [zs[c].astype(BF16) for c in chunks]
    lhs_v = [jnp.concatenate([jnp.where(strict, s_k[c][0:RW_CHUNK], 0.0),
                              jnp.where(incl, s_k[c][RW_CHUNK:2 * RW_CHUNK], 0.0)], axis=0).astype(BF16)
             for c in chunks]
    res_v = [_dot(lhs_v[c], bd(vb[sls[c]])) for c in chunks]
    yield
    p1 = [_dot(tinv[c], bd(kap[sls[c]])) for c in chunks]
    p2 = [_dot(tinv[c], bd(res_v[c][0:RW_CHUNK].astype(BF16))) for c in chunks]
    yield
    a_rb = [jnp.where(incl, s_b[c][RW_CHUNK:2 * RW_CHUNK], 0.0).astype(BF16) for c in chunks]
    y0 = [res_v[c][RW_CHUNK:2 * RW_CHUNK] - _dot(a_rb[c], bd(p2[c].astype(BF16))) for c in chunks]
    qq = [rt[sls[c]] - _dot(a_rb[c], bd(p1[c].astype(BF16))) for c in chunks]
    yield

    s_val = s_st[...]
    ys = []
    for c in chunks:
        sl = sls[c]
        lhs = jnp.concatenate([p1[c], qq[c]], axis=0)
        res = _bdot(lhs, s_val, _NT)
        u_c = res[0:RW_CHUNK] + p2[c]
        ys.append(y0[c] + res[RW_CHUNK:2 * RW_CHUNK])
        lt = jnp.concatenate([vb[sl], (-u_c).astype(BF16)], axis=0)
        rt2 = jnp.concatenate([kbar[sl], bbar[sl]], axis=0)
        s_val = s_val * jnp.exp(lc_last[c]) + jnp.where(blockdiag, _dot(lt, rt2, _TN), 0.0)
        yield
    s_st[...] = s_val
    y = jnp.concatenate(ys, axis=0)

    yn = _head_norm(y, gng_ref[...], gnb_ref[...], hms)
    yield
    bonus = _head_sum(r * k * rk_ref[...], hms) * v
    y_ref[0] = (yn + bonus) * g


def _tspec(width):
    return pl.BlockSpec((1, TB, width), lambda b, t: (b, t, 0))


def _cspec(shape):
    return pl.BlockSpec(shape, lambda b, t: (0,) * len(shape))


def _fused_mixer_kernel(parts, *refs):
    n_in = sum(p[1] for p in parts)
    n_out = sum(p[2] for p in parts)
    ins, outs, scr = refs[:n_in], refs[n_in:n_in + n_out], refs[n_in + n_out:]
    for phase in (_INIT, _BODY):
        gens = []
        i = o = s = 0
        for fn, ni, no, ns in parts:
            gens.append(fn(phase, *ins[i:i + ni], *outs[o:o + no], *scr[s:s + ns]))
            i, o, s = i + ni, o + no, s + ns
        for _ in range(_LEAD_STAGES):
            next(gens[0], _DONE)
        while gens:
            for g in list(gens):
                if next(g, _DONE) is _DONE:
                    gens.remove(g)


def _mixer_call(name, batch, tp, parts):
    grid = (batch, tp // TB)
    in_specs, operands, out_shape, out_specs, scratch, sig = [], [], [], [], [], []
    for fn, time_ins, const_ins, out_widths, scr in parts:
        in_specs += [_tspec(a.shape[2]) for a in time_ins] + [_cspec(a.shape) for a in const_ins]
        operands += list(time_ins) + list(const_ins)
        out_shape += [jax.ShapeDtypeStruct((batch, tp, w), F32) for w in out_widths]
        out_specs += [_tspec(w) for w in out_widths]
        scratch += list(scr)
        sig.append((fn, len(time_ins) + len(const_ins), len(out_widths), len(scr)))
    return pl.pallas_call(
        functools.partial(_fused_mixer_kernel, tuple(sig)),
        grid=grid,
        in_specs=in_specs,
        out_specs=out_specs,
        out_shape=out_shape,
        scratch_shapes=scratch,
        compiler_params=pltpu.CompilerParams(dimension_semantics=("arbitrary", "arbitrary"),
                                             vmem_limit_bytes=VMEM_LIMIT),
        name=name,
    )(*operands)


def _block_diag(w):
    eye = jnp.eye(N_H, dtype=w.dtype)
    return jnp.einsum('gcd,gh->gchd', w, eye).reshape(GROUP_W, GROUP_W)


def _pad_rows(w, total, offset):
    return jnp.zeros((total, w.shape[1]), w.dtype).at[offset:offset + w.shape[0]].set(w)


def _row(v):
    return v.reshape(1, -1)


@jax.jit
def kernel(x, meta, emb_ln_g, emb_ln_b, w_in, w_out, pool_w, pool_scale, rw_mu, rw_w0, rw_w_up, rw_a0, rw_a_up, rw_g_up, rw_k_k, rw_k_a, rw_r_k, rw_gn_g, rw_gn_b, rw_v0, rw_v_down, rw_v_up, lru_conv_w, lru_conv_b, lru_ga_w, lru_ga_b, lru_gx_w, lru_gx_b, lru_lambda, ml_if_b, ml_gn_g, ml_gn_b, ln1_g, ln1_b, ln2_g, ln2_b, mlp_w1, mlp_w2):
    batch, seq, _ = x.shape
    depth = w_in.shape[0]
    assert seq % TB == 0 and x.shape[2] == D_MODEL
    tp = TB + seq
    rows = batch * tp
    alpha = (2 * depth) ** 0.25

    front = jnp.zeros((batch, TB - N_META, D_MODEL), x.dtype)
    metab = jnp.broadcast_to(meta[None].astype(x.dtype), (batch, N_META, D_MODEL))
    hcat = jnp.concatenate([front, metab, x], axis=1).reshape(rows, D_MODEL)
    h = _rows_call(_embed_ln_kernel, "embed_ln", rows, [hcat], [_row(emb_ln_g), _row(emb_ln_b)], [D_MODEL])

    w_in_b = jnp.pad(w_in, ((0, 0), (0, 0), (0, D_IN_PAD - w_in.shape[2]))).astype(BF16)
    w_out_b = w_out.astype(BF16)
    w1_b = mlp_w1.astype(BF16)
    w2_b = mlp_w2.astype(BF16)

    v_first = None
    for l in range(depth):
        p_pool, p_rw, p_lru, p_ml = _rows_call(
            _inproj_kernel, "in_proj", rows, [h], [w_in_b[l]], [GROUP_W, RW_COLS, 2 * GROUP_W, ML_COLS_PAD])
        p_pool = p_pool.reshape(batch, tp, GROUP_W)
        p_rw = p_rw.reshape(batch, tp, RW_COLS)
        p_lru = p_lru.reshape(batch, tp, 2 * GROUP_W)
        p_ml = p_ml.reshape(batch, tp, ML_COLS_PAD)

        wax = jnp.concatenate([_block_diag(lru_ga_w[l]), _block_diag(lru_gx_w[l])], axis=1).astype(BF16)
        bax = jnp.concatenate([lru_ga_b[l], lru_gx_b[l]]).reshape(1, 2 * GROUP_W)
        pool_lru_part = (
            _pool_lru_kernel, [p_pool, p_lru],
            [_block_diag(pool_w[l]).astype(BF16), _row(pool_scale[l]), lru_conv_w[l], _row(lru_conv_b[l]),
             wax, bax, _row(lru_lambda[l])],
            [GROUP_W, GROUP_W],
            [pltpu.VMEM((TB + POOL_MAX_W, GROUP_W), F32), pltpu.VMEM((TB + SUBLANE, GROUP_W), F32),
             pltpu.VMEM((TB + _SCAN_PAD, GROUP_W), F32), pltpu.VMEM((TB + _SCAN_PAD, GROUP_W), F32),
             pltpu.VMEM((1, GROUP_W), F32)])

        ifb = jnp.zeros((1, LANE), F32).at[0, 0:2 * N_H].set(ml_if_b[l])
        mlstm_part = (
            _mlstm_kernel, [p_ml], [ifb, _row(ml_gn_g[l]), _row(ml_gn_b[l])], [GROUP_W],
            [pltpu.VMEM((GROUP_W, GROUP_W), F32), pltpu.VMEM((1, GROUP_W), F32), pltpu.VMEM((1, LANE), F32)])

        wup = _pad_rows(rw_w_up[l], LANE, 0).astype(BF16)
        aup = _pad_rows(rw_a_up[l], LANE, 32).astype(BF16)
        gup = _pad_rows(rw_g_up[l], LANE, 64).astype(BF16)
        rw_consts = [_row(rw_mu[l]), _row(rw_w0[l]), wup, _row(rw_a0[l]), aup, gup, _row(rw_k_k[l]),
                     _row(rw_k_a[l]), _row(rw_r_k[l]), _row(rw_gn_g[l]), _row(rw_gn_b[l])]
        rw_scratch = [pltpu.VMEM((TB + SUBLANE, RW_COLS), F32), pltpu.VMEM((GROUP_W, GROUP_W), F32)]
        if l == 0:
            rwkv_part = (functools.partial(_rwkv_kernel, False), [p_rw], rw_consts, [GROUP_W, GROUP_W],
                         rw_scratch)
            y_rw, v_first, y_ml, y_pool, y_lru = _mixer_call(
                "mixers_first", batch, tp, [rwkv_part, mlstm_part, pool_lru_part])
        else:
            vdn = jnp.pad(rw_v_down[l - 1], ((0, 0), (0, LANE - rw_v_down.shape[2]))).astype(BF16)
            vup = _pad_rows(rw_v_up[l - 1], LANE, 0).astype(BF16)
            rwkv_part = (functools.partial(_rwkv_kernel, True), [p_rw, v_first],
                         rw_consts + [_row(rw_v0[l - 1]), vdn, vup], [GROUP_W], rw_scratch)
            y_rw, y_ml, y_pool, y_lru = _mixer_call("mixers", batch, tp, [rwkv_part, mlstm_part, pool_lru_part])

        ys = [y.reshape(rows, GROUP_W) for y in (y_pool, y_rw, y_lru, y_ml)]
        h = _rows_call(functools.partial(_outproj_kernel, alpha), "out_proj", rows, [h] + ys,
                       [w_out_b[l], _row(ln1_g[l]), _row(ln1_b[l])], [D_MODEL])
        h = _rows_call(functools.partial(_mlp_kernel, alpha), "mlp", rows, [h],
                       [w1_b[l], w2_b[l], _row(ln2_g[l]), _row(ln2_b[l])], [D_MODEL])

    return h.reshape(batch, tp, D_MODEL)[:, TB:, :]
```

```python
import functools
import math

import jax
import jax.numpy as jnp
from jax import lax
from jax.experimental import pallas as pl
from jax.experimental.pallas import tpu as pltpu

D_MODEL = 1024
N_META = 16
GROUP_W = 256
HEAD = 64
N_H = 4
POOL_MAX_W = 16
CONV_W = 4
LRU_C = 8.0
LN_EPS = 1e-5
GN_EPS = 64e-5
NEG = -1e30
D_FF = 4096
RW_COLS = 896
ML_COLS_PAD = 1152
D_IN_PAD = 2816
LANE = 128
SUBLANE = 8

TB = 256
RW_CHUNK = 64
TM = 512
VMEM_LIMIT = 56 * 1024 * 1024

F32 = jnp.float32
BF16 = jnp.bfloat16

_NN = (((1,), (0,)), ((), ()))
_NT = (((1,), (1,)), ((), ()))
_TN = (((0,), (0,)), ((), ()))


def _dot(a, b, dims=_NN):
    return lax.dot_general(a, b, dims, preferred_element_type=F32)


def _bdot(a, b, dims=_NN):
    return _dot(a.astype(BF16), b.astype(BF16), dims)


def _split3(x):
    hi = x.astype(BF16)
    r1 = x - hi.astype(F32)
    mid = r1.astype(BF16)
    lo = (r1 - mid.astype(F32)).astype(BF16)
    return hi, mid, lo


def _dot_exact_lhs(e_bf16, x, dims=_NN):
    hi, mid, lo = _split3(x)
    return _dot(e_bf16, hi, dims) + _dot(e_bf16, mid, dims) + _dot(e_bf16, lo, dims)


def _softplus(x):
    return jnp.maximum(x, 0.0) + jnp.log(1.0 + jnp.exp(-jnp.abs(x)))


def _sigmoid(x):
    return 0.5 * jnp.tanh(0.5 * x) + 0.5


def _softplus_small(x):
    return jnp.maximum(x, 0.0) + jnp.log1p(jnp.exp(-jnp.abs(x)))


def _layer_norm(x, g, b):
    mu = jnp.mean(x, axis=-1, keepdims=True)
    xc = x - mu
    var = jnp.mean(xc * xc, axis=-1, keepdims=True)
    return xc * lax.rsqrt(var + LN_EPS) * g + b


def _head_masks(width=GROUP_W):
    lane = lax.broadcasted_iota(jnp.int32, (1, width), 1)
    return [(lane // HEAD) == h for h in range(N_H)]


def _head_ones():
    rr = lax.broadcasted_iota(jnp.int32, (GROUP_W, 1), 0) // HEAD
    cc = lax.broadcasted_iota(jnp.int32, (1, GROUP_W), 1) // HEAD
    same = rr == cc
    return same, jnp.where(same, 1.0, 0.0).astype(BF16)


def _head_sum(x, ones_b):
    return _dot(x.astype(BF16), ones_b)


def _head_norm(y, g, b, ones_b):
    mu = _head_sum(y, ones_b) * (1.0 / HEAD)
    yc = y - mu
    var = _head_sum(yc * yc, ones_b) * (1.0 / HEAD)
    return yc * lax.rsqrt(var + GN_EPS) * g + b


def _col(x, j):
    lane = lax.broadcasted_iota(jnp.int32, (1, x.shape[1]), 1)
    return jnp.sum(jnp.where(lane == j, x, 0.0), axis=1, keepdims=True)


def _valid_rows(t):
    row = t * TB + lax.broadcasted_iota(jnp.int32, (TB, 1), 0)
    return row, row >= (TB - N_META)


def _embed_ln_kernel(x_ref, meta_ref, g_ref, b_ref, o_ref):
    t = pl.program_id(1)

    @pl.when(t == 0)
    def _():
        o_ref[0, 0:TB - N_META, :] = jnp.zeros((TB - N_META, D_MODEL), F32)
        o_ref[0, TB - N_META:TB, :] = _layer_norm(meta_ref[...], g_ref[...], b_ref[...])

    @pl.when(t > 0)
    def _():
        o_ref[0] = _layer_norm(x_ref[0], g_ref[...], b_ref[...])


def _inproj_kernel(h_ref, w_ref, o_pool, o_rw, o_lru, o_ml):
    hb = h_ref[...].astype(BF16)
    o_pool[...] = _dot(hb, w_ref[:, 0:256])
    o_rw[...] = _dot(hb, w_ref[:, 256:1152])
    o_lru[...] = _dot(hb, w_ref[:, 1152:1664])
    o_ml[...] = _dot(hb, w_ref[:, 1664:D_IN_PAD])


def _post_kernel(alpha, h_ref, y0, y1, y2, y3, wo_ref, g1_ref, b1_ref, w1_ref, w2_ref, g2_ref, b2_ref, o_ref):
    mix = _dot(y0[...].astype(BF16), wo_ref[0:256, :])
    mix += _dot(y1[...].astype(BF16), wo_ref[256:512, :])
    mix += _dot(y2[...].astype(BF16), wo_ref[512:768, :])
    mix += _dot(y3[...].astype(BF16), wo_ref[768:1024, :])
    h = _layer_norm(alpha * h_ref[...] + mix, g1_ref[...], b1_ref[...])
    hb = h.astype(BF16)
    acc = jnp.zeros_like(h)
    for j in range(D_FF // D_MODEL):
        cs = slice(j * D_MODEL, (j + 1) * D_MODEL)
        z = jnp.maximum(_dot(hb, w1_ref[:, cs]), 0.0)
        acc += _dot((z * z).astype(BF16), w2_ref[cs, :])
    o_ref[...] = _layer_norm(alpha * h + acc, g2_ref[...], b2_ref[...])


def _row_spec(width):
    return pl.BlockSpec((TM, width), lambda i: (i, 0))


def _const_spec(shape):
    return pl.BlockSpec(shape, lambda i: (0,) * len(shape), pipeline_mode=pl.Buffered(1))


def _rows_call(kernel, name, rows, row_ins, const_ins, out_widths):
    n = pl.cdiv(rows, TM)
    in_specs = [_row_spec(a.shape[1]) for a in row_ins] + [_const_spec(a.shape) for a in const_ins]
    out_shape = [jax.ShapeDtypeStruct((rows, w), F32) for w in out_widths]
    out_specs = [_row_spec(w) for w in out_widths]
    single = len(out_widths) == 1
    res = pl.pallas_call(
        kernel,
        grid=(n,),
        in_specs=in_specs,
        out_specs=out_specs[0] if single else out_specs,
        out_shape=out_shape[0] if single else out_shape,
        compiler_params=pltpu.CompilerParams(dimension_semantics=("arbitrary",), vmem_limit_bytes=VMEM_LIMIT),
        name=name,
    )(*row_ins, *const_ins)
    return res


_SCAN_PAD = TB // 2
_POOL_OFF = SUBLANE + POOL_MAX_W
_INIT, _BODY = "init", "body"
_DONE = object()
_LEAD_STAGES = 5


def _pool_lru_kernel(phase, pp_ref, plru_ref, poolw_ref, pscale_ref, convw_ref, convb_ref, wax_ref, bax_ref,
                     lam_ref, ypool_ref, ylru_ref, pext, pw2, pw4, pw8, cext, s_a, s_u, hcar):
    t = pl.program_id(0)
    if phase == _INIT:
        @pl.when(t == 0)
        def _():
            pext[0:_POOL_OFF, :] = jnp.zeros((_POOL_OFF, GROUP_W), F32)
            for buf in (pw2, pw4, pw8):
                buf[0:SUBLANE, :] = jnp.zeros((SUBLANE, GROUP_W), F32)
            cext[0:SUBLANE, :] = jnp.zeros((SUBLANE, GROUP_W), F32)
            hcar[...] = jnp.zeros((1, GROUP_W), F32)
            s_a[0:_SCAN_PAD, :] = jnp.ones((_SCAN_PAD, GROUP_W), F32)
            s_u[0:_SCAN_PAD, :] = jnp.zeros((_SCAN_PAD, GROUP_W), F32)
        return

    row, valid = _valid_rows(t)

    u = jnp.where(valid, pp_ref[...], 0.0)
    lo_p, hi_p = SUBLANE, _POOL_OFF + TB
    pext[_POOL_OFF:hi_p, :] = u
    pw2[lo_p:hi_p, :] = pext[lo_p:hi_p, :] + pext[lo_p - 1:hi_p - 1, :]
    yield
    pw4[lo_p:hi_p, :] = pw2[lo_p:hi_p, :] + pw2[lo_p - 2:hi_p - 2, :]
    yield
    pw8[lo_p:hi_p, :] = pw4[lo_p:hi_p, :] + pw4[lo_p - 4:hi_p - 4, :]
    yield
    w16 = pw8[_POOL_OFF:hi_p, :] + pw8[_POOL_OFF - 8:hi_p - 8, :]
    grp = lax.broadcasted_iota(jnp.int32, (1, GROUP_W), 1) // HEAD
    acc = jnp.where(grp == 0, pw2[_POOL_OFF:hi_p, :],
                    jnp.where(grp == 1, pw4[_POOL_OFF:hi_p, :],
                              jnp.where(grp == 2, pw8[_POOL_OFF:hi_p, :], w16)))
    win = jnp.left_shift(2, grp)
    pos1 = (row - (TB - N_META) + 1).astype(F32)
    cnt = jnp.clip(pos1, 1.0, win.astype(F32))
    dlt = acc / cnt - u
    ypool_ref[...] = _bdot(dlt, poolw_ref[...]) * pscale_ref[...]
    pext[lo_p:_POOL_OFF, :] = pext[TB + lo_p:TB + _POOL_OFF, :]
    yield

    x = jnp.where(valid, plru_ref[:, 0:GROUP_W], 0.0)
    gate = plru_ref[:, GROUP_W:2 * GROUP_W]
    cext[SUBLANE:SUBLANE + TB, :] = x
    xc = jnp.zeros((TB, GROUP_W), F32) + convb_ref[...]
    for j in range(CONV_W):
        off = SUBLANE - (CONV_W - 1) + j
        xc = xc + convw_ref[j:j + 1, :] * cext[off:off + TB, :]
    cext[0:SUBLANE, :] = cext[TB:TB + SUBLANE, :]
    yield
    z = _bdot(xc, wax_ref[...]) + bax_ref[...]
    r = _sigmoid(z[:, 0:GROUP_W])
    i = _sigmoid(z[:, GROUP_W:2 * GROUP_W])
    yield
    log_a = (-LRU_C) * r * _softplus_small(-lam_ref[...])
    a = jnp.exp(log_a)
    uu = jnp.where(valid, jnp.sqrt(-jnp.tanh(log_a) * (a * a + 1.0)) * (i * xc), 0.0)
    lo, hi = _SCAN_PAD, _SCAN_PAD + TB
    s_a[lo:hi, :] = a
    s_u[lo:hi, :] = uu
    yield
    d = 1
    while d < TB:
        a0, u0 = s_a[lo:hi, :], s_u[lo:hi, :]
        a1, u1 = s_a[lo - d:hi - d, :], s_u[lo - d:hi - d, :]
        s_u[lo:hi, :] = a0 * u1 + u0
        s_a[lo:hi, :] = a0 * a1
        d *= 2
        yield
    h = s_u[lo:hi, :] + s_a[lo:hi, :] * hcar[...]
    hcar[...] = h[TB - 1:TB, :]
    yield
    gcube = gate * gate * gate
    gelu = 0.5 * gate * (1.0 + jnp.tanh(math.sqrt(2.0 / math.pi) * (gate + 0.044715 * gcube)))
    ylru_ref[...] = h * gelu


def _mlstm_kernel(phase, p_ref, ifb_ref, gng_ref, gnb_ref, y_ref, c_st, n_st, m_st):
    t = pl.program_id(0)
    if phase == _INIT:
        @pl.when(t == 0)
        def _():
            c_st[...] = jnp.zeros((GROUP_W, GROUP_W), F32)
            n_st[...] = jnp.zeros((1, GROUP_W), F32)
            m_st[...] = jnp.zeros((1, LANE), F32)
        return

    _, valid = _valid_rows(t)
    hms = _head_masks()
    q = jnp.where(valid, p_ref[:, 0:256], 0.0) * (HEAD ** -0.5)
    k = jnp.where(valid, p_ref[:, 256:512], 0.0)
    v = jnp.where(valid, p_ref[:, 512:768], 0.0)
    o = p_ref[:, 768:1024]
    g = p_ref[:, 1024:ML_COLS_PAD] + ifb_ref[...]
    lane = lax.broadcasted_iota(jnp.int32, (1, LANE), 1)
    is_i = lane < N_H
    is_f = jnp.logical_and(lane >= N_H, lane < 2 * N_H)
    logf = jnp.where(jnp.logical_and(is_f, valid), -_softplus(-g), 0.0)
    logi = jnp.where(is_i, jnp.where(valid, g, NEG), 0.0)

    tcol = lax.broadcasted_iota(jnp.int32, (TB, 1), 0)
    srow = lax.broadcasted_iota(jnp.int32, (1, TB), 1)
    causal = srow <= tcol
    tri = jnp.where(causal, 1.0, 0.0).astype(BF16)
    bcum = _dot_exact_lhs(tri, logf)
    srow8 = lax.broadcasted_iota(jnp.int32, (SUBLANE, LANE), 0)
    lane8 = lax.broadcasted_iota(jnp.int32, (SUBLANE, LANE), 1)
    psel = (jnp.where(lane8 == srow8, 1.0, 0.0) - jnp.where(lane8 == srow8 + N_H, 1.0, 0.0)).astype(BF16)
    yield
    zt = _dot_exact_lhs(psel, logi + bcum, _NT)
    yield

    m_prev = m_st[...]
    m_inter_t = bcum + m_prev
    kb, vb = k.astype(BF16), v.astype(BF16)
    qc = _bdot(q, c_st[...])
    qn = q * n_st[...]

    num = jnp.zeros((TB, GROUP_W), F32)
    wk_e = jnp.zeros((TB, GROUP_W), F32)
    sc_row = jnp.zeros((1, GROUP_W), F32)
    m_new_t = jnp.zeros((1, LANE), F32)
    for h in range(N_H):
        hm = hms[h]
        b_h = _col(bcum, N_H + h)
        dmat = jnp.where(causal, b_h + zt[h:h + 1, :], NEG)
        m_intra = jnp.max(dmat, axis=1, keepdims=True)
        m_inter = _col(m_inter_t, N_H + h)
        m_t = jnp.maximum(m_intra, m_inter)
        sc = _dot(jnp.where(hm, q, 0.0).astype(BF16), kb, _NT)
        pm = sc * jnp.exp(dmat - m_t)
        yield
        num_h = _dot(pm.astype(BF16), jnp.where(hm, v, 0.0).astype(BF16))
        w_int = jnp.exp(m_inter - m_t)
        den = (jnp.sum(pm, axis=1, keepdims=True)
               + w_int * jnp.sum(jnp.where(hm, qn, 0.0), axis=1, keepdims=True))
        denom = jnp.maximum(jnp.abs(den), jnp.exp(-m_t))
        num = num + (num_h + w_int * jnp.where(hm, qc, 0.0)) * (1.0 / denom)
        yield
        b_last = b_h[TB - 1:TB, :]
        g_loc = b_last + (_col(logi, h) - b_h)
        m_loc = jnp.max(g_loc, axis=0, keepdims=True)
        m_old = jnp.sum(jnp.where(lane == N_H + h, m_prev, 0.0), axis=1, keepdims=True)
        m_new = jnp.maximum(b_last + m_old, m_loc)
        wk_e = wk_e + jnp.where(hm, jnp.exp(g_loc - m_new), 0.0)
        sc_row = sc_row + jnp.where(hm, jnp.exp(b_last + m_old - m_new), 0.0)
        m_new_t = m_new_t + jnp.where(lane == N_H + h, m_new, 0.0)
        yield

    kw = k * wk_e
    blockdiag, ones_b = _head_ones()
    c_st[...] = sc_row * c_st[...] + jnp.where(blockdiag, _dot(kw.astype(BF16), vb, _TN), 0.0)
    n_st[...] = sc_row * n_st[...] + jnp.sum(kw, axis=0, keepdims=True)
    m_st[...] = m_new_t
    yield

    y_ref[...] = _head_norm(num, gng_ref[...], gnb_ref[...], ones_b) * _sigmoid(o)


def _rwkv_kernel(has_vfirst, phase, *refs):
    if has_vfirst:
        (p_ref, vf_ref, mu_ref, w0_ref, wup_ref, a0_ref, aup_ref, gup_ref, kk_ref, ka_ref, rk_ref, gng_ref,
         gnb_ref, v0_ref, vdn_ref, vup_ref, y_ref, pext, s_st) = refs
    else:
        (p_ref, mu_ref, w0_ref, wup_ref, a0_ref, aup_ref, gup_ref, kk_ref, ka_ref, rk_ref, gng_ref,
         gnb_ref, y_ref, vfo_ref, pext, s_st) = refs
    t = pl.program_id(0)
    if phase == _INIT:
        @pl.when(t == 0)
        def _():
            pext[0:SUBLANE, :] = jnp.zeros((SUBLANE, RW_COLS), F32)
            s_st[...] = jnp.zeros((GROUP_W, GROUP_W), F32)
        return

    _, valid = _valid_rows(t)
    blockdiag, bd_b = _head_ones()
    p = jnp.where(valid, p_ref[...], 0.0)
    pext[SUBLANE:SUBLANE + TB, :] = p
    prev = pext[SUBLANE - 1:SUBLANE - 1 + TB, :]
    pext[0:SUBLANE, :] = pext[TB:TB + SUBLANE, :]
    pm = p + (prev - p) * mu_ref[...]
    yield
    r = pm[:, 0:256]
    k = pm[:, 256:512]
    v = pm[:, 512:768]
    lora = pm[:, 768:RW_COLS]
    ld = (-math.exp(-0.5)) * _sigmoid(w0_ref[...] + _bdot(jnp.tanh(lora), wup_ref[...]))
    a = _sigmoid(a0_ref[...] + _bdot(lora, aup_ref[...]))
    g = _bdot(_sigmoid(lora), gup_ref[...])
    yield
    if has_vfirst:
        vmix = _sigmoid(v0_ref[...] + _bdot(_bdot(v, vdn_ref[...]), vup_ref[...]))
        v = v + (vf_ref[...] - v) * vmix
    else:
        vfo_ref[...] = v
    kk = k * kk_ref[...]
    kk = kk * lax.rsqrt(_head_sum(kk * kk, bd_b) + 1e-12)
    k = k * (1.0 + (a - 1.0) * ka_ref[...])
    bvec = kk * a
    yield

    tcol = lax.broadcasted_iota(jnp.int32, (TB, 1), 0)
    srow = lax.broadcasted_iota(jnp.int32, (1, TB), 1)
    cum_mask = jnp.logical_and((tcol // RW_CHUNK) == (srow // RW_CHUNK), srow <= tcol)
    lc = _dot_exact_lhs(jnp.where(cum_mask, 1.0, 0.0).astype(BF16), ld)
    n_chunks = TB // RW_CHUNK
    chunks = range(n_chunks)
    lc_last = [lc[(c + 1) * RW_CHUNK - 1:(c + 1) * RW_CHUNK, :] for c in chunks]
    lc_end = jnp.concatenate([jnp.broadcast_to(x, (RW_CHUNK, GROUP_W)) for x in lc_last], axis=0)
    yield
    e_neg = jnp.exp(-lc)
    kap = (kk * jnp.exp(lc - ld)).astype(BF16)
    kt = (k * e_neg).astype(BF16)
    bt = (bvec * e_neg).astype(BF16)
    rt = r * jnp.exp(lc)
    rtb = rt.astype(BF16)
    vb = v.astype(BF16)
    e_end = jnp.exp(lc_end - lc)
    kbar = (k * e_end).astype(BF16)
    bbar = (bvec * e_end).astype(BF16)
    yield

    def bd(xb):
        return jnp.concatenate([xb] * N_H, axis=0) * bd_b

    tloc = lax.broadcasted_iota(jnp.int32, (RW_CHUNK, 1), 0)
    sloc = lax.broadcasted_iota(jnp.int32, (1, GROUP_W), 1) % RW_CHUNK
    strict = sloc < tloc
    incl = sloc <= tloc
    eye_pk = jnp.where(sloc == tloc, 1.0, 0.0)
    sls = [slice(c * RW_CHUNK, (c + 1) * RW_CHUNK) for c in chunks]

    lhs_kr = [jnp.concatenate([kap[sl], rtb[sl]], axis=0) for sl in sls]
    s_k = [_dot(lhs_kr[c], bd(kt[sls[c]]), _NT) for c in chunks]
    s_b = [_dot(lhs_kr[c], bd(bt[sls[c]]), _NT) for c in chunks]
    yield
    nmat = [jnp.where(strict, -s_b[c][0:RW_CHUNK], 0.0) for c in chunks]
    zs = [eye_pk + nmat[c] for c in chunks]
    ypow = [nmat[c].astype(BF16) for c in chunks]
    ypow = [_dot(ypow[c], bd(ypow[c])).astype(BF16) for c in chunks]
    n_iter = int(math.log2(RW_CHUNK)) - 1
    yield
    for it in range(n_iter):
        rhs = [bd(ypow[c]) for c in chunks]
        if it + 1 < n_iter:
            res = [_dot(jnp.concatenate([ypow[c], zs[c].astype(BF16)], axis=0), rhs[c]) for c in chunks]
            ypow = [res[c][0:RW_CHUNK].astype(BF16) for c in chunks]
            zs = [zs[c] + res[c][RW_CHUNK:2 * RW_CHUNK] for c in chunks]
        else:
            zs = [zs[c] + _dot(zs[c].astype(BF16), rhs[c]) for c in chunks]
        yield
    tinv = [zs[c].astype(BF16) for c in chunks]
    lhs_v = [jnp.concatenate([jnp.where(strict, s_k[c][0:RW_CHUNK], 0.0),
                              jnp.where(incl, s_k[c][RW_CHUNK:2 * RW_CHUNK], 0.0)], axis=0).astype(BF16)
             for c in chunks]
    res_v = [_dot(lhs_v[c], bd(vb[sls[c]])) for c in chunks]
    yield
    p1 = [_dot(tinv[c], bd(kap[sls[c]])) for c in chunks]
    p2 = [_dot(tinv[c], bd(res_v[c][0:RW_CHUNK].astype(BF16))) for c in chunks]
    yield
    a_rb = [jnp.where(incl, s_b[c][RW_CHUNK:2 * RW_CHUNK], 0.0).astype(BF16) for c in chunks]
    y0 = [res_v[c][RW_CHUNK:2 * RW_CHUNK] - _dot(a_rb[c], bd(p2[c].astype(BF16))) for c in chunks]
    qq = [rt[sls[c]] - _dot(a_rb[c], bd(p1[c].astype(BF16))) for c in chunks]
    yield

    s_val = s_st[...]
    ys = []
    for c in chunks:
        sl = sls[c]
        lhs = jnp.concatenate([p1[c], qq[c]], axis=0)
        res = _bdot(lhs, s_val, _NT)
        u_c = res[0:RW_CHUNK] + p2[c]
        ys.append(y0[c] + res[RW_CHUNK:2 * RW_CHUNK])
        lt = jnp.concatenate([vb[sl], (-u_c).astype(BF16)], axis=0)
        rt2 = jnp.concatenate([kbar[sl], bbar[sl]], axis=0)
        s_val = s_val * jnp.exp(lc_last[c]) + jnp.where(blockdiag, _dot(lt, rt2, _TN), 0.0)
        yield
    s_st[...] = s_val
    y = jnp.concatenate(ys, axis=0)

    yn = _head_norm(y, gng_ref[...], gnb_ref[...], bd_b)
    yield
    bonus = _head_sum(r * k * rk_ref[...], bd_b) * v
    y_ref[...] = (yn + bonus) * g


def _tspec(batch, width):
    return pl.BlockSpec((batch, TB, width), lambda t: (0, t, 0))


def _cspec(shape):
    return pl.BlockSpec(shape, lambda t: (0,) * len(shape))


def _fused_mixer_kernel(parts, batch, *refs):
    n_in = sum(p[1] + p[2] for p in parts)
    n_out = sum(p[3] for p in parts)
    ins, outs, scr = refs[:n_in], refs[n_in:n_in + n_out], refs[n_in + n_out:]
    for phase in (_INIT, _BODY):
        gens = []
        i = o = s = 0
        for fn, nt, nc, no, ns in parts:
            for b in range(batch):
                args = [r.at[b] for r in ins[i:i + nt]] + list(ins[i + nt:i + nt + nc])
                args += [r.at[b] for r in outs[o:o + no]] + [r.at[b] for r in scr[s:s + ns]]
                gens.append(fn(phase, *args))
            i, o, s = i + nt + nc, o + no, s + ns
        for _ in range(_LEAD_STAGES):
            for g in gens[:batch]:
                next(g, _DONE)
        while gens:
            for g in list(gens):
                if next(g, _DONE) is _DONE:
                    gens.remove(g)


def _mixer_call(name, batch, tp, parts):
    in_specs, operands, out_shape, out_specs, scratch, sig = [], [], [], [], [], []
    for fn, time_ins, const_ins, out_widths, scr in parts:
        in_specs += [_tspec(batch, a.shape[2]) for a in time_ins] + [_cspec(a.shape) for a in const_ins]
        operands += list(time_ins) + list(const_ins)
        out_shape += [jax.ShapeDtypeStruct((batch, tp, w), F32) for w in out_widths]
        out_specs += [_tspec(batch, w) for w in out_widths]
        scratch += [pltpu.VMEM((batch,) + tuple(shape), F32) for shape in scr]
        sig.append((fn, len(time_ins), len(const_ins), len(out_widths), len(scr)))
    return pl.pallas_call(
        functools.partial(_fused_mixer_kernel, tuple(sig), batch),
        grid=(tp // TB,),
        in_specs=in_specs,
        out_specs=out_specs,
        out_shape=out_shape,
        scratch_shapes=scratch,
        compiler_params=pltpu.CompilerParams(dimension_semantics=("arbitrary",),
                                             vmem_limit_bytes=VMEM_LIMIT),
        name=name,
    )(*operands)


def _block_diag(w):
    eye = jnp.eye(N_H, dtype=w.dtype)
    return jnp.einsum('gcd,gh->gchd', w, eye).reshape(GROUP_W, GROUP_W)


def _pad_rows(w, total, offset):
    return jnp.zeros((total, w.shape[1]), w.dtype).at[offset:offset + w.shape[0]].set(w)


def _row(v):
    return v.reshape(1, -1)


@jax.jit
def kernel(x, meta, emb_ln_g, emb_ln_b, w_in, w_out, pool_w, pool_scale, rw_mu, rw_w0, rw_w_up, rw_a0, rw_a_up, rw_g_up, rw_k_k, rw_k_a, rw_r_k, rw_gn_g, rw_gn_b, rw_v0, rw_v_down, rw_v_up, lru_conv_w, lru_conv_b, lru_ga_w, lru_ga_b, lru_gx_w, lru_gx_b, lru_lambda, ml_if_b, ml_gn_g, ml_gn_b, ln1_g, ln1_b, ln2_g, ln2_b, mlp_w1, mlp_w2):
    batch, seq, _ = x.shape
    depth = w_in.shape[0]
    assert seq % TB == 0 and x.shape[2] == D_MODEL
    tp = TB + seq
    rows = batch * tp
    alpha = (2 * depth) ** 0.25

    h = pl.pallas_call(
        _embed_ln_kernel,
        grid=(batch, tp // TB),
        in_specs=[pl.BlockSpec((1, TB, D_MODEL), lambda b, t: (b, jnp.maximum(t - 1, 0), 0)),
                  pl.BlockSpec((N_META, D_MODEL), lambda b, t: (0, 0)),
                  pl.BlockSpec((1, D_MODEL), lambda b, t: (0, 0)),
                  pl.BlockSpec((1, D_MODEL), lambda b, t: (0, 0))],
        out_specs=pl.BlockSpec((1, TB, D_MODEL), lambda b, t: (b, t, 0)),
        out_shape=jax.ShapeDtypeStruct((batch, tp, D_MODEL), F32),
        compiler_params=pltpu.CompilerParams(dimension_semantics=("arbitrary", "arbitrary")),
        name="embed_ln",
    )(x, meta.astype(x.dtype), _row(emb_ln_g), _row(emb_ln_b)).reshape(rows, D_MODEL)

    w_in_b = jnp.pad(w_in, ((0, 0), (0, 0), (0, D_IN_PAD - w_in.shape[2]))).astype(BF16)
    w_out_b = w_out.astype(BF16)
    w1_b = mlp_w1.astype(BF16)
    w2_b = mlp_w2.astype(BF16)

    v_first = None
    for l in range(depth):
        p_pool, p_rw, p_lru, p_ml = _rows_call(
            _inproj_kernel, "in_proj", rows, [h], [w_in_b[l]], [GROUP_W, RW_COLS, 2 * GROUP_W, ML_COLS_PAD])
        p_pool = p_pool.reshape(batch, tp, GROUP_W)
        p_rw = p_rw.reshape(batch, tp, RW_COLS)
        p_lru = p_lru.reshape(batch, tp, 2 * GROUP_W)
        p_ml = p_ml.reshape(batch, tp, ML_COLS_PAD)

        wax = jnp.concatenate([_block_diag(lru_ga_w[l]), _block_diag(lru_gx_w[l])], axis=1).astype(BF16)
        bax = jnp.concatenate([lru_ga_b[l], lru_gx_b[l]]).reshape(1, 2 * GROUP_W)
        pool_lru_part = (
            _pool_lru_kernel, [p_pool, p_lru],
            [_block_diag(pool_w[l]).astype(BF16), _row(pool_scale[l]), lru_conv_w[l], _row(lru_conv_b[l]),
             wax, bax, _row(lru_lambda[l])],
            [GROUP_W, GROUP_W],
            [(TB + _POOL_OFF, GROUP_W)] * 4 + [(TB + SUBLANE, GROUP_W), (TB + _SCAN_PAD, GROUP_W),
                                               (TB + _SCAN_PAD, GROUP_W), (1, GROUP_W)])

        ifb = jnp.zeros((1, LANE), F32).at[0, 0:2 * N_H].set(ml_if_b[l])
        mlstm_part = (
            _mlstm_kernel, [p_ml], [ifb, _row(ml_gn_g[l]), _row(ml_gn_b[l])], [GROUP_W],
            [(GROUP_W, GROUP_W), (1, GROUP_W), (1, LANE)])

        wup = _pad_rows(rw_w_up[l], LANE, 0).astype(BF16)
        aup = _pad_rows(rw_a_up[l], LANE, 32).astype(BF16)
        gup = _pad_rows(rw_g_up[l], LANE, 64).astype(BF16)
        rw_consts = [_row(rw_mu[l]), _row(rw_w0[l]), wup, _row(rw_a0[l]), aup, gup, _row(rw_k_k[l]),
                     _row(rw_k_a[l]), _row(rw_r_k[l]), _row(rw_gn_g[l]), _row(rw_gn_b[l])]
        rw_scratch = [(TB + SUBLANE, RW_COLS), (GROUP_W, GROUP_W)]
        if l == 0:
            rwkv_part = (functools.partial(_rwkv_kernel, False), [p_rw], rw_consts, [GROUP_W, GROUP_W],
                         rw_scratch)
            y_rw, v_first, y_ml, y_pool, y_lru = _mixer_call(
                "mixers_first", batch, tp, [rwkv_part, mlstm_part, pool_lru_part])
        else:
            vdn = jnp.pad(rw_v_down[l - 1], ((0, 0), (0, LANE - rw_v_down.shape[2]))).astype(BF16)
            vup = _pad_rows(rw_v_up[l - 1], LANE, 0).astype(BF16)
            rwkv_part = (functools.partial(_rwkv_kernel, True), [p_rw, v_first],
                         rw_consts + [_row(rw_v0[l - 1]), vdn, vup], [GROUP_W], rw_scratch)
            y_rw, y_ml, y_pool, y_lru = _mixer_call("mixers", batch, tp, [rwkv_part, mlstm_part, pool_lru_part])

        ys = [y.reshape(rows, GROUP_W) for y in (y_pool, y_rw, y_lru, y_ml)]
        h = _rows_call(functools.partial(_post_kernel, alpha), "out_proj_mlp", rows, [h] + ys,
                       [w_out_b[l], _row(ln1_g[l]), _row(ln1_b[l]), w1_b[l], w2_b[l], _row(ln2_g[l]),
                        _row(ln2_b[l])], [D_MODEL])

    return h.reshape(batch, tp, D_MODEL)[:, TB:, :]
```

```python
import functools
import math

import jax
import jax.numpy as jnp
from jax import lax
from jax.experimental import pallas as pl
from jax.experimental.pallas import tpu as pltpu

D_MODEL = 1024
N_META = 16
GROUP_W = 256
HEAD = 64
N_H = 4
POOL_MAX_W = 16
CONV_W = 4
LRU_C = 8.0
LN_EPS = 1e-5
GN_EPS = 64e-5
NEG = -1e30
D_FF = 4096
RW_COLS = 896
ML_COLS_PAD = 1152
D_IN_PAD = 2816
LANE = 128
SUBLANE = 8

TB = 256
RW_CHUNK = 64
TM = 512
VMEM_LIMIT = 56 * 1024 * 1024

F32 = jnp.float32
BF16 = jnp.bfloat16

_NN = (((1,), (0,)), ((), ()))
_NT = (((1,), (1,)), ((), ()))
_TN = (((0,), (0,)), ((), ()))


def _dot(a, b, dims=_NN):
    return lax.dot_general(a, b, dims, preferred_element_type=F32)


def _bdot(a, b, dims=_NN):
    return _dot(a.astype(BF16), b.astype(BF16), dims)


def _split3(x):
    hi = x.astype(BF16)
    r1 = x - hi.astype(F32)
    mid = r1.astype(BF16)
    lo = (r1 - mid.astype(F32)).astype(BF16)
    return hi, mid, lo


def _dot_exact_lhs(e_bf16, x, dims=_NN):
    hi, mid, lo = _split3(x)
    return _dot(e_bf16, hi, dims) + _dot(e_bf16, mid, dims) + _dot(e_bf16, lo, dims)


def _softplus(x):
    return jnp.maximum(x, 0.0) + jnp.log(1.0 + jnp.exp(-jnp.abs(x)))


def _sigmoid(x):
    return 0.5 * jnp.tanh(0.5 * x) + 0.5


def _softplus_small(x):
    return jnp.maximum(x, 0.0) + jnp.log1p(jnp.exp(-jnp.abs(x)))


def _layer_norm(x, g, b):
    mu = jnp.mean(x, axis=-1, keepdims=True)
    xc = x - mu
    var = jnp.mean(xc * xc, axis=-1, keepdims=True)
    return xc * lax.rsqrt(var + LN_EPS) * g + b


def _head_masks(width=GROUP_W):
    lane = lax.broadcasted_iota(jnp.int32, (1, width), 1)
    return [(lane // HEAD) == h for h in range(N_H)]


def _head_ones():
    rr = lax.broadcasted_iota(jnp.int32, (GROUP_W, 1), 0) // HEAD
    cc = lax.broadcasted_iota(jnp.int32, (1, GROUP_W), 1) // HEAD
    same = rr == cc
    return same, jnp.where(same, 1.0, 0.0).astype(BF16)


def _head_sum(x, ones_b):
    return _dot(x.astype(BF16), ones_b)


def _head_norm(y, g, b, ones_b):
    mu = _head_sum(y, ones_b) * (1.0 / HEAD)
    yc = y - mu
    var = _head_sum(yc * yc, ones_b) * (1.0 / HEAD)
    return yc * lax.rsqrt(var + GN_EPS) * g + b


def _col(x, j):
    lane = lax.broadcasted_iota(jnp.int32, (1, x.shape[1]), 1)
    return jnp.sum(jnp.where(lane == j, x, 0.0), axis=1, keepdims=True)


def _valid_rows(t):
    row = t * TB + lax.broadcasted_iota(jnp.int32, (TB, 1), 0)
    return row, row >= (TB - N_META)


def _embed_ln_kernel(x_ref, meta_ref, g_ref, b_ref, o_ref):
    t = pl.program_id(1)

    @pl.when(t == 0)
    def _():
        o_ref[0, 0:TB - N_META, :] = jnp.zeros((TB - N_META, D_MODEL), F32)
        o_ref[0, TB - N_META:TB, :] = _layer_norm(meta_ref[...], g_ref[...], b_ref[...])

    @pl.when(t > 0)
    def _():
        o_ref[0] = _layer_norm(x_ref[0], g_ref[...], b_ref[...])


def _post_kernel(alpha, h_ref, y0, y1, y2, y3, wo_ref, g1_ref, b1_ref, w1_ref, w2_ref, g2_ref, b2_ref, o_ref):
    mix = _dot(y0[...].astype(BF16), wo_ref[0:256, :])
    mix += _dot(y1[...].astype(BF16), wo_ref[256:512, :])
    mix += _dot(y2[...].astype(BF16), wo_ref[512:768, :])
    mix += _dot(y3[...].astype(BF16), wo_ref[768:1024, :])
    h = _layer_norm(alpha * h_ref[...] + mix, g1_ref[...], b1_ref[...])
    hb = h.astype(BF16)
    acc = jnp.zeros_like(h)
    for j in range(D_FF // D_MODEL):
        cs = slice(j * D_MODEL, (j + 1) * D_MODEL)
        z = jnp.maximum(_dot(hb, w1_ref[:, cs]), 0.0)
        acc += _dot((z * z).astype(BF16), w2_ref[cs, :])
    o_ref[...] = _layer_norm(alpha * h + acc, g2_ref[...], b2_ref[...])


def _row_spec(width):
    return pl.BlockSpec((TM, width), lambda i: (i, 0))


def _const_spec(shape):
    return pl.BlockSpec(shape, lambda i: (0,) * len(shape), pipeline_mode=pl.Buffered(1))


def _rows_call(kernel, name, rows, row_ins, const_ins, out_widths):
    n = pl.cdiv(rows, TM)
    in_specs = [_row_spec(a.shape[1]) for a in row_ins] + [_const_spec(a.shape) for a in const_ins]
    out_shape = [jax.ShapeDtypeStruct((rows, w), F32) for w in out_widths]
    out_specs = [_row_spec(w) for w in out_widths]
    single = len(out_widths) == 1
    res = pl.pallas_call(
        kernel,
        grid=(n,),
        in_specs=in_specs,
        out_specs=out_specs[0] if single else out_specs,
        out_shape=out_shape[0] if single else out_shape,
        compiler_params=pltpu.CompilerParams(dimension_semantics=("arbitrary",), vmem_limit_bytes=VMEM_LIMIT),
        name=name,
    )(*row_ins, *const_ins)
    return res


_SCAN_PAD = TB // 2
_POOL_OFF = SUBLANE + POOL_MAX_W
_INIT, _BODY = "init", "body"
_DONE = object()
_LEAD_STAGES = 5
_RWKV_STAGES, _MLSTM_STAGES, _POOL_LRU_STAGES, _INPROJ_STAGES = 22, 16, 17, 11


def _pool_lru_kernel(phase, pp_ref, plru_ref, poolw_ref, pscale_ref, convw_ref, convb_ref, wax_ref, bax_ref,
                     lam_ref, ypool_ref, ylru_ref, pext, pw2, pw4, pw8, cext, gbuf, s_a, s_u, hcar):
    t = pl.program_id(0)
    if phase == _INIT:
        @pl.when(t == 0)
        def _():
            pext[0:_POOL_OFF, :] = jnp.zeros((_POOL_OFF, GROUP_W), F32)
            for buf in (pw2, pw4, pw8):
                buf[0:SUBLANE, :] = jnp.zeros((SUBLANE, GROUP_W), F32)
            cext[0:SUBLANE, :] = jnp.zeros((SUBLANE, GROUP_W), F32)
            hcar[...] = jnp.zeros((1, GROUP_W), F32)
            s_a[0:_SCAN_PAD, :] = jnp.ones((_SCAN_PAD, GROUP_W), F32)
            s_u[0:_SCAN_PAD, :] = jnp.zeros((_SCAN_PAD, GROUP_W), F32)
        return

    row, valid = _valid_rows(t)

    u = jnp.where(valid, pp_ref[...], 0.0)
    lo_p, hi_p = SUBLANE, _POOL_OFF + TB
    pext[_POOL_OFF:hi_p, :] = u
    cext[SUBLANE:SUBLANE + TB, :] = jnp.where(valid, plru_ref[:, 0:GROUP_W], 0.0)
    gbuf[...] = plru_ref[:, GROUP_W:2 * GROUP_W]
    pw2[lo_p:hi_p, :] = pext[lo_p:hi_p, :] + pext[lo_p - 1:hi_p - 1, :]
    yield
    pw4[lo_p:hi_p, :] = pw2[lo_p:hi_p, :] + pw2[lo_p - 2:hi_p - 2, :]
    yield
    pw8[lo_p:hi_p, :] = pw4[lo_p:hi_p, :] + pw4[lo_p - 4:hi_p - 4, :]
    yield
    w16 = pw8[_POOL_OFF:hi_p, :] + pw8[_POOL_OFF - 8:hi_p - 8, :]
    grp = lax.broadcasted_iota(jnp.int32, (1, GROUP_W), 1) // HEAD
    acc = jnp.where(grp == 0, pw2[_POOL_OFF:hi_p, :],
                    jnp.where(grp == 1, pw4[_POOL_OFF:hi_p, :],
                              jnp.where(grp == 2, pw8[_POOL_OFF:hi_p, :], w16)))
    win = jnp.left_shift(2, grp)
    pos1 = (row - (TB - N_META) + 1).astype(F32)
    cnt = jnp.clip(pos1, 1.0, win.astype(F32))
    dlt = acc / cnt - u
    ypool_ref[...] = _bdot(dlt, poolw_ref[...]) * pscale_ref[...]
    pext[lo_p:_POOL_OFF, :] = pext[TB + lo_p:TB + _POOL_OFF, :]
    yield

    xc = jnp.zeros((TB, GROUP_W), F32) + convb_ref[...]
    for j in range(CONV_W):
        off = SUBLANE - (CONV_W - 1) + j
        xc = xc + convw_ref[j:j + 1, :] * cext[off:off + TB, :]
    cext[0:SUBLANE, :] = cext[TB:TB + SUBLANE, :]
    yield
    z = _bdot(xc, wax_ref[...]) + bax_ref[...]
    r = _sigmoid(z[:, 0:GROUP_W])
    i = _sigmoid(z[:, GROUP_W:2 * GROUP_W])
    yield
    log_a = (-LRU_C) * r * _softplus_small(-lam_ref[...])
    a = jnp.exp(log_a)
    uu = jnp.where(valid, jnp.sqrt(-jnp.tanh(log_a) * (a * a + 1.0)) * (i * xc), 0.0)
    lo, hi = _SCAN_PAD, _SCAN_PAD + TB
    s_a[lo:hi, :] = a
    s_u[lo:hi, :] = uu
    yield
    d = 1
    while d < TB:
        a0, u0 = s_a[lo:hi, :], s_u[lo:hi, :]
        a1, u1 = s_a[lo - d:hi - d, :], s_u[lo - d:hi - d, :]
        s_u[lo:hi, :] = a0 * u1 + u0
        s_a[lo:hi, :] = a0 * a1
        d *= 2
        yield
    h = s_u[lo:hi, :] + s_a[lo:hi, :] * hcar[...]
    hcar[...] = h[TB - 1:TB, :]
    yield
    gate = gbuf[...]
    gcube = gate * gate * gate
    gelu = 0.5 * gate * (1.0 + jnp.tanh(math.sqrt(2.0 / math.pi) * (gate + 0.044715 * gcube)))
    ylru_ref[...] = h * gelu


def _mlstm_kernel(phase, p_ref, ifb_ref, gng_ref, gnb_ref, y_ref, c_st, n_st, m_st):
    t = pl.program_id(0)
    if phase == _INIT:
        @pl.when(t == 0)
        def _():
            c_st[...] = jnp.zeros((GROUP_W, GROUP_W), F32)
            n_st[...] = jnp.zeros((1, GROUP_W), F32)
            m_st[...] = jnp.zeros((1, LANE), F32)
        return

    _, valid = _valid_rows(t)
    hms = _head_masks()
    q = jnp.where(valid, p_ref[:, 0:256], 0.0) * (HEAD ** -0.5)
    k = jnp.where(valid, p_ref[:, 256:512], 0.0)
    v = jnp.where(valid, p_ref[:, 512:768], 0.0)
    o = p_ref[:, 768:1024]
    g = p_ref[:, 1024:ML_COLS_PAD] + ifb_ref[...]
    lane = lax.broadcasted_iota(jnp.int32, (1, LANE), 1)
    is_i = lane < N_H
    is_f = jnp.logical_and(lane >= N_H, lane < 2 * N_H)
    logf = jnp.where(jnp.logical_and(is_f, valid), -_softplus(-g), 0.0)
    logi = jnp.where(is_i, jnp.where(valid, g, NEG), 0.0)

    tcol = lax.broadcasted_iota(jnp.int32, (TB, 1), 0)
    srow = lax.broadcasted_iota(jnp.int32, (1, TB), 1)
    causal = srow <= tcol
    tri = jnp.where(causal, 1.0, 0.0).astype(BF16)
    bcum = _dot_exact_lhs(tri, logf)
    srow8 = lax.broadcasted_iota(jnp.int32, (SUBLANE, LANE), 0)
    lane8 = lax.broadcasted_iota(jnp.int32, (SUBLANE, LANE), 1)
    psel = (jnp.where(lane8 == srow8, 1.0, 0.0) - jnp.where(lane8 == srow8 + N_H, 1.0, 0.0)).astype(BF16)
    yield
    zt = _dot_exact_lhs(psel, logi + bcum, _NT)
    yield

    m_prev = m_st[...]
    m_inter_t = bcum + m_prev
    kb, vb = k.astype(BF16), v.astype(BF16)
    qc = _bdot(q, c_st[...])
    qn = q * n_st[...]

    num = jnp.zeros((TB, GROUP_W), F32)
    wk_e = jnp.zeros((TB, GROUP_W), F32)
    sc_row = jnp.zeros((1, GROUP_W), F32)
    m_new_t = jnp.zeros((1, LANE), F32)
    for h in range(N_H):
        hm = hms[h]
        b_h = _col(bcum, N_H + h)
        dmat = jnp.where(causal, b_h + zt[h:h + 1, :], NEG)
        m_intra = jnp.max(dmat, axis=1, keepdims=True)
        m_inter = _col(m_inter_t, N_H + h)
        m_t = jnp.maximum(m_intra, m_inter)
        sc = _dot(jnp.where(hm, q, 0.0).astype(BF16), kb, _NT)
        pm = sc * jnp.exp(dmat - m_t)
        yield
        num_h = _dot(pm.astype(BF16), jnp.where(hm, v, 0.0).astype(BF16))
        w_int = jnp.exp(m_inter - m_t)
        den = (jnp.sum(pm, axis=1, keepdims=True)
               + w_int * jnp.sum(jnp.where(hm, qn, 0.0), axis=1, keepdims=True))
        denom = jnp.maximum(jnp.abs(den), jnp.exp(-m_t))
        num = num + (num_h + w_int * jnp.where(hm, qc, 0.0)) * (1.0 / denom)
        yield
        b_last = b_h[TB - 1:TB, :]
        g_loc = b_last + (_col(logi, h) - b_h)
        m_loc = jnp.max(g_loc, axis=0, keepdims=True)
        m_old = jnp.sum(jnp.where(lane == N_H + h, m_prev, 0.0), axis=1, keepdims=True)
        m_new = jnp.maximum(b_last + m_old, m_loc)
        wk_e = wk_e + jnp.where(hm, jnp.exp(g_loc - m_new), 0.0)
        sc_row = sc_row + jnp.where(hm, jnp.exp(b_last + m_old - m_new), 0.0)
        m_new_t = m_new_t + jnp.where(lane == N_H + h, m_new, 0.0)
        yield

    kw = k * wk_e
    blockdiag, ones_b = _head_ones()
    c_st[...] = sc_row * c_st[...] + jnp.where(blockdiag, _dot(kw.astype(BF16), vb, _TN), 0.0)
    n_st[...] = sc_row * n_st[...] + jnp.sum(kw, axis=0, keepdims=True)
    m_st[...] = m_new_t
    yield

    y_ref[...] = _head_norm(num, gng_ref[...], gnb_ref[...], ones_b) * _sigmoid(o)


def _rwkv_kernel(has_vfirst, phase, *refs):
    if has_vfirst:
        (p_ref, vf_ref, mu_ref, w0_ref, wup_ref, a0_ref, aup_ref, gup_ref, kk_ref, ka_ref, rk_ref, gng_ref,
         gnb_ref, v0_ref, vdn_ref, vup_ref, y_ref, pext, s_st) = refs
    else:
        (p_ref, mu_ref, w0_ref, wup_ref, a0_ref, aup_ref, gup_ref, kk_ref, ka_ref, rk_ref, gng_ref,
         gnb_ref, y_ref, vfo_ref, pext, s_st) = refs
    t = pl.program_id(0)
    if phase == _INIT:
        @pl.when(t == 0)
        def _():
            pext[0:SUBLANE, :] = jnp.zeros((SUBLANE, RW_COLS), F32)
            s_st[...] = jnp.zeros((GROUP_W, GROUP_W), F32)
        return

    _, valid = _valid_rows(t)
    blockdiag, bd_b = _head_ones()
    p = jnp.where(valid, p_ref[...], 0.0)
    pext[SUBLANE:SUBLANE + TB, :] = p
    prev = pext[SUBLANE - 1:SUBLANE - 1 + TB, :]
    pext[0:SUBLANE, :] = pext[TB:TB + SUBLANE, :]
    pm = p + (prev - p) * mu_ref[...]
    yield
    r = pm[:, 0:256]
    k = pm[:, 256:512]
    v = pm[:, 512:768]
    lora = pm[:, 768:RW_COLS]
    ld = (-math.exp(-0.5)) * _sigmoid(w0_ref[...] + _bdot(jnp.tanh(lora), wup_ref[...]))
    a = _sigmoid(a0_ref[...] + _bdot(lora, aup_ref[...]))
    g = _bdot(_sigmoid(lora), gup_ref[...])
    yield
    if has_vfirst:
        vmix = _sigmoid(v0_ref[...] + _bdot(_bdot(v, vdn_ref[...]), vup_ref[...]))
        v = v + (vf_ref[...] - v) * vmix
    else:
        vfo_ref[...] = v
    kk = k * kk_ref[...]
    kk = kk * lax.rsqrt(_head_sum(kk * kk, bd_b) + 1e-12)
    k = k * (1.0 + (a - 1.0) * ka_ref[...])
    bvec = kk * a
    yield

    tcol = lax.broadcasted_iota(jnp.int32, (TB, 1), 0)
    srow = lax.broadcasted_iota(jnp.int32, (1, TB), 1)
    cum_mask = jnp.logical_and((tcol // RW_CHUNK) == (srow // RW_CHUNK), srow <= tcol)
    lc = _dot_exact_lhs(jnp.where(cum_mask, 1.0, 0.0).astype(BF16), ld)
    n_chunks = TB // RW_CHUNK
    chunks = range(n_chunks)
    lc_last = [lc[(c + 1) * RW_CHUNK - 1:(c + 1) * RW_CHUNK, :] for c in chunks]
    lc_end = jnp.concatenate([jnp.broadcast_to(x, (RW_CHUNK, GROUP_W)) for x in lc_last], axis=0)
    yield
    e_neg = jnp.exp(-lc)
    kap = (kk * jnp.exp(lc - ld)).astype(BF16)
    kt = (k * e_neg).astype(BF16)
    bt = (bvec * e_neg).astype(BF16)
    rt = r * jnp.exp(lc)
    rtb = rt.astype(BF16)
    vb = v.astype(BF16)
    e_end = jnp.exp(lc_end - lc)
    kbar = (k * e_end).astype(BF16)
    bbar = (bvec * e_end).astype(BF16)
    yield

    def bd(xb):
        return jnp.concatenate([xb] * N_H, axis=0) * bd_b

    tloc = lax.broadcasted_iota(jnp.int32, (RW_CHUNK, 1), 0)
    sloc = lax.broadcasted_iota(jnp.int32, (1, GROUP_W), 1) % RW_CHUNK
    strict = sloc < tloc
    incl = sloc <= tloc
    eye_pk = jnp.where(sloc == tloc, 1.0, 0.0)
    sls = [slice(c * RW_CHUNK, (c + 1) * RW_CHUNK) for c in chunks]

    lhs_kr = [jnp.concatenate([kap[sl], rtb[sl]], axis=0) for sl in sls]
    s_k = [_dot(lhs_kr[c], bd(kt[sls[c]]), _NT) for c in chunks]
    s_b = [_dot(lhs_kr[c], bd(bt[sls[c]]), _NT) for c in chunks]
    yield
    nmat = [jnp.where(strict, -s_b[c][0:RW_CHUNK], 0.0) for c in chunks]
    zs = [eye_pk + nmat[c] for c in chunks]
    ypow = [nmat[c].astype(BF16) for c in chunks]
    ypow = [_dot(ypow[c], bd(ypow[c])).astype(BF16) for c in chunks]
    n_iter = int(math.log2(RW_CHUNK)) - 1
    yield
    for it in range(n_iter):
        rhs = [bd(ypow[c]) for c in chunks]
        if it + 1 < n_iter:
            res = [_dot(jnp.concatenate([ypow[c], zs[c].astype(BF16)], axis=0), rhs[c]) for c in chunks]
            ypow = [res[c][0:RW_CHUNK].astype(BF16) for c in chunks]
            zs = [zs[c] + res[c][RW_CHUNK:2 * RW_CHUNK] for c in chunks]
        else:
            zs = [zs[c] + _dot(zs[c].astype(BF16), rhs[c]) for c in chunks]
        yield
    tinv = [zs[c].astype(BF16) for c in chunks]
    lhs_v = [jnp.concatenate([jnp.where(strict, s_k[c][0:RW_CHUNK], 0.0),
                              jnp.where(incl, s_k[c][RW_CHUNK:2 * RW_CHUNK], 0.0)], axis=0).astype(BF16)
             for c in chunks]
    res_v = [_dot(lhs_v[c], bd(vb[sls[c]])) for c in chunks]
    yield
    p1 = [_dot(tinv[c], bd(kap[sls[c]])) for c in chunks]
    p2 = [_dot(tinv[c], bd(res_v[c][0:RW_CHUNK].astype(BF16))) for c in chunks]
    yield
    a_rb = [jnp.where(incl, s_b[c][RW_CHUNK:2 * RW_CHUNK], 0.0).astype(BF16) for c in chunks]
    y0 = [res_v[c][RW_CHUNK:2 * RW_CHUNK] - _dot(a_rb[c], bd(p2[c].astype(BF16))) for c in chunks]
    qq = [rt[sls[c]] - _dot(a_rb[c], bd(p1[c].astype(BF16))) for c in chunks]
    yield

    s_val = s_st[...]
    ys = []
    for c in chunks:
        sl = sls[c]
        lhs = jnp.concatenate([p1[c], qq[c]], axis=0)
        res = _bdot(lhs, s_val, _NT)
        u_c = res[0:RW_CHUNK] + p2[c]
        ys.append(y0[c] + res[RW_CHUNK:2 * RW_CHUNK])
        lt = jnp.concatenate([vb[sl], (-u_c).astype(BF16)], axis=0)
        rt2 = jnp.concatenate([kbar[sl], bbar[sl]], axis=0)
        s_val = s_val * jnp.exp(lc_last[c]) + jnp.where(blockdiag, _dot(lt, rt2, _TN), 0.0)
        yield
    s_st[...] = s_val
    y = jnp.concatenate(ys, axis=0)

    yn = _head_norm(y, gng_ref[...], gnb_ref[...], bd_b)
    yield
    bonus = _head_sum(r * k * rk_ref[...], bd_b) * v
    y_ref[...] = (yn + bonus) * g


def _tspec(batch, width):
    return pl.BlockSpec((batch, TB, width), lambda t: (0, t, 0))


def _cspec(shape):
    return pl.BlockSpec(shape, lambda t: (0,) * len(shape), pipeline_mode=pl.Buffered(1))


_P_SEGMENTS = (("pool", 0, 256), ("rw", 256, 1152), ("lru", 1152, 1664), ("ml", 1664, D_IN_PAD))


def _inproj_stages(h_ref, w_ref, p_refs, batch):
    hb = jnp.concatenate([h_ref[b] for b in range(batch)], axis=0).astype(BF16)
    for name, lo, hi in _P_SEGMENTS:
        for c0 in range(lo, hi, GROUP_W):
            c1 = min(c0 + GROUP_W, hi)
            res = _dot(hb, w_ref[:, c0:c1])
            for b in range(batch):
                p_refs[name][b, :, c0 - lo:c1 - lo] = res[b * TB:(b + 1) * TB]
            yield


def _fused_mixer_kernel(parts, batch, *refs):
    h_next_ref, h_first_ref, w_ref = refs[:3]
    refs = refs[3:]
    n_in = sum(p[3] + p[4] for p in parts)
    n_out = sum(p[5] for p in parts)
    ins, outs, scr = refs[:n_in], refs[n_in:n_in + n_out], refs[n_in + n_out:]
    p_refs = {name: r for (name, _, _), r in zip(_P_SEGMENTS, scr)}
    scr = scr[len(_P_SEGMENTS):]

    @pl.when(pl.program_id(0) == 0)
    def _():
        for _ in _inproj_stages(h_first_ref, w_ref, p_refs, batch):
            pass

    for phase in (_INIT, _BODY):
        gens, totals = [], []
        i = o = s = 0
        for fn, n_stages, pnames, nt, nc, no, ns in parts:
            for b in range(batch):
                args = [p_refs[n].at[b] for n in pnames] + [r.at[b] for r in ins[i:i + nt]]
                args += list(ins[i + nt:i + nt + nc])
                args += [r.at[b] for r in outs[o:o + no]] + [r.at[b] for r in scr[s:s + ns]]
                gens.append(fn(phase, *args))
                totals.append(n_stages)
            i, o, s = i + nt + nc, o + no, s + ns
        if phase == _BODY:
            gens.append(_inproj_stages(h_next_ref, w_ref, p_refs, batch))
            totals.append(_INPROJ_STAGES)
        done = [0] * len(gens)

        def advance(j):
            if next(gens[j], _DONE) is _DONE:
                done[j] = None
            else:
                done[j] += 1

        for _ in range(_LEAD_STAGES):
            for j in range(batch):
                advance(j)
        for j in range(len(gens)):
            if done[j] == 0:
                advance(j)
        while any(d is not None for d in done):
            j = min((d / totals[k], k) for k, d in enumerate(done) if d is not None)[1]
            advance(j)


def _mixer_call(name, batch, tp, h, w_in_b, parts):
    n_t = tp // TB
    in_specs = [pl.BlockSpec((batch, TB, D_MODEL), lambda t: (0, jnp.minimum(t + 1, n_t - 1), 0)),
                pl.BlockSpec((batch, TB, D_MODEL), lambda t: (0, 0, 0), pipeline_mode=pl.Buffered(1)),
                _cspec(w_in_b.shape)]
    operands = [h, h, w_in_b]
    out_shape, out_specs, sig = [], [], []
    scratch = [pltpu.VMEM((batch, TB, hi - lo), F32) for _, lo, hi in _P_SEGMENTS]
    for fn, n_stages, pnames, time_ins, const_ins, out_widths, scr in parts:
        in_specs += [_tspec(batch, a.shape[2]) for a in time_ins] + [_cspec(a.shape) for a in const_ins]
        operands += list(time_ins) + list(const_ins)
        out_shape += [jax.ShapeDtypeStruct((batch, tp, w), F32) for w in out_widths]
        out_specs += [_tspec(batch, w) for w in out_widths]
        scratch += [pltpu.VMEM((batch,) + tuple(shape), F32) for shape in scr]
        sig.append((fn, n_stages, tuple(pnames), len(time_ins), len(const_ins), len(out_widths), len(scr)))
    return pl.pallas_call(
        functools.partial(_fused_mixer_kernel, tuple(sig), batch),
        grid=(n_t,),
        in_specs=in_specs,
        out_specs=out_specs,
        out_shape=out_shape,
        scratch_shapes=scratch,
        compiler_params=pltpu.CompilerParams(dimension_semantics=("arbitrary",),
                                             vmem_limit_bytes=VMEM_LIMIT),
        name=name,
    )(*operands)


def _block_diag(w):
    eye = jnp.eye(N_H, dtype=w.dtype)
    return jnp.einsum('gcd,gh->gchd', w, eye).reshape(GROUP_W, GROUP_W)


def _pad_rows(w, total, offset):
    return jnp.zeros((total, w.shape[1]), w.dtype).at[offset:offset + w.shape[0]].set(w)


def _row(v):
    return v.reshape(1, -1)


@jax.jit
def kernel(x, meta, emb_ln_g, emb_ln_b, w_in, w_out, pool_w, pool_scale, rw_mu, rw_w0, rw_w_up, rw_a0, rw_a_up, rw_g_up, rw_k_k, rw_k_a, rw_r_k, rw_gn_g, rw_gn_b, rw_v0, rw_v_down, rw_v_up, lru_conv_w, lru_conv_b, lru_ga_w, lru_ga_b, lru_gx_w, lru_gx_b, lru_lambda, ml_if_b, ml_gn_g, ml_gn_b, ln1_g, ln1_b, ln2_g, ln2_b, mlp_w1, mlp_w2):
    batch, seq, _ = x.shape
    depth = w_in.shape[0]
    assert seq % TB == 0 and x.shape[2] == D_MODEL
    tp = TB + seq
    rows = batch * tp
    alpha = (2 * depth) ** 0.25

    h = pl.pallas_call(
        _embed_ln_kernel,
        grid=(batch, tp // TB),
        in_specs=[pl.BlockSpec((1, TB, D_MODEL), lambda b, t: (b, jnp.maximum(t - 1, 0), 0)),
                  pl.BlockSpec((N_META, D_MODEL), lambda b, t: (0, 0)),
                  pl.BlockSpec((1, D_MODEL), lambda b, t: (0, 0)),
                  pl.BlockSpec((1, D_MODEL), lambda b, t: (0, 0))],
        out_specs=pl.BlockSpec((1, TB, D_MODEL), lambda b, t: (b, t, 0)),
        out_shape=jax.ShapeDtypeStruct((batch, tp, D_MODEL), F32),
        compiler_params=pltpu.CompilerParams(dimension_semantics=("arbitrary", "arbitrary")),
        name="embed_ln",
    )(x, meta.astype(x.dtype), _row(emb_ln_g), _row(emb_ln_b)).reshape(rows, D_MODEL)

    w_in_b = jnp.pad(w_in, ((0, 0), (0, 0), (0, D_IN_PAD - w_in.shape[2]))).astype(BF16)
    w_out_b = w_out.astype(BF16)
    w1_b = mlp_w1.astype(BF16)
    w2_b = mlp_w2.astype(BF16)

    v_first = None
    for l in range(depth):
        h3 = h.reshape(batch, tp, D_MODEL)
        wax = jnp.concatenate([_block_diag(lru_ga_w[l]), _block_diag(lru_gx_w[l])], axis=1).astype(BF16)
        bax = jnp.concatenate([lru_ga_b[l], lru_gx_b[l]]).reshape(1, 2 * GROUP_W)
        pool_lru_part = (
            _pool_lru_kernel, _POOL_LRU_STAGES, ("pool", "lru"), [],
            [_block_diag(pool_w[l]).astype(BF16), _row(pool_scale[l]), lru_conv_w[l], _row(lru_conv_b[l]),
             wax, bax, _row(lru_lambda[l])],
            [GROUP_W, GROUP_W],
            [(TB + _POOL_OFF, GROUP_W)] * 4 + [(TB + SUBLANE, GROUP_W), (TB, GROUP_W),
                                               (TB + _SCAN_PAD, GROUP_W), (TB + _SCAN_PAD, GROUP_W),
                                               (1, GROUP_W)])

        ifb = jnp.zeros((1, LANE), F32).at[0, 0:2 * N_H].set(ml_if_b[l])
        mlstm_part = (
            _mlstm_kernel, _MLSTM_STAGES, ("ml",), [], [ifb, _row(ml_gn_g[l]), _row(ml_gn_b[l])], [GROUP_W],
            [(GROUP_W, GROUP_W), (1, GROUP_W), (1, LANE)])

        wup = _pad_rows(rw_w_up[l], LANE, 0).astype(BF16)
        aup = _pad_rows(rw_a_up[l], LANE, 32).astype(BF16)
        gup = _pad_rows(rw_g_up[l], LANE, 64).astype(BF16)
        rw_consts = [_row(rw_mu[l]), _row(rw_w0[l]), wup, _row(rw_a0[l]), aup, gup, _row(rw_k_k[l]),
                     _row(rw_k_a[l]), _row(rw_r_k[l]), _row(rw_gn_g[l]), _row(rw_gn_b[l])]
        rw_scratch = [(TB + SUBLANE, RW_COLS), (GROUP_W, GROUP_W)]
        if l == 0:
            rwkv_part = (functools.partial(_rwkv_kernel, False), _RWKV_STAGES, ("rw",), [], rw_consts,
                         [GROUP_W, GROUP_W], rw_scratch)
            y_rw, v_first, y_ml, y_pool, y_lru = _mixer_call(
                "mixers_first", batch, tp, h3, w_in_b[l], [rwkv_part, mlstm_part, pool_lru_part])
        else:
            vdn = jnp.pad(rw_v_down[l - 1], ((0, 0), (0, LANE - rw_v_down.shape[2]))).astype(BF16)
            vup = _pad_rows(rw_v_up[l - 1], LANE, 0).astype(BF16)
            rwkv_part = (functools.partial(_rwkv_kernel, True), _RWKV_STAGES, ("rw",), [v_first],
                         rw_consts + [_row(rw_v0[l - 1]), vdn, vup], [GROUP_W], rw_scratch)
            y_rw, y_ml, y_pool, y_lru = _mixer_call(
                "mixers", batch, tp, h3, w_in_b[l], [rwkv_part, mlstm_part, pool_lru_part])

        ys = [y.reshape(rows, GROUP_W) for y in (y_pool, y_rw, y_lru, y_ml)]
        h = _rows_call(functools.partial(_post_kernel, alpha), "out_proj_mlp", rows, [h] + ys,
                       [w_out_b[l], _row(ln1_g[l]), _row(ln1_b[l]), w1_b[l], w2_b[l], _row(ln2_g[l]),
                        _row(ln2_b[l])], [D_MODEL])

    return h.reshape(batch, tp, D_MODEL)[:, TB:, :]
```

```python
import functools
import math

import jax
import jax.numpy as jnp
from jax import lax
from jax.experimental import pallas as pl
from jax.experimental.pallas import tpu as pltpu

D_MODEL = 1024
N_META = 16
GROUP_W = 256
HEAD = 64
N_H = 4
POOL_MAX_W = 16
CONV_W = 4
LRU_C = 8.0
LN_EPS = 1e-5
GN_EPS = 64e-5
NEG = -1e30
D_FF = 4096
RW_COLS = 896
ML_COLS_PAD = 1152
D_IN_PAD = 2816
LANE = 128
SUBLANE = 8

TB = 256
RW_CHUNK = 64
VMEM_LIMIT = 60 * 1024 * 1024

F32 = jnp.float32
BF16 = jnp.bfloat16

_NN = (((1,), (0,)), ((), ()))
_NT = (((1,), (1,)), ((), ()))
_TN = (((0,), (0,)), ((), ()))


def _dot(a, b, dims=_NN):
    return lax.dot_general(a, b, dims, preferred_element_type=F32)


def _bdot(a, b, dims=_NN):
    return _dot(a.astype(BF16), b.astype(BF16), dims)


def _split3(x):
    hi = x.astype(BF16)
    r1 = x - hi.astype(F32)
    mid = r1.astype(BF16)
    lo = (r1 - mid.astype(F32)).astype(BF16)
    return hi, mid, lo


def _dot_exact_lhs(e_bf16, x, dims=_NN):
    hi, mid, lo = _split3(x)
    return _dot(e_bf16, hi, dims) + _dot(e_bf16, mid, dims) + _dot(e_bf16, lo, dims)


def _softplus(x):
    return jnp.maximum(x, 0.0) + jnp.log(1.0 + jnp.exp(-jnp.abs(x)))


def _sigmoid(x):
    return 0.5 * jnp.tanh(0.5 * x) + 0.5


def _softplus_small(x):
    return jnp.maximum(x, 0.0) + jnp.log1p(jnp.exp(-jnp.abs(x)))


def _layer_norm(x, g, b):
    mu = jnp.mean(x, axis=-1, keepdims=True)
    xc = x - mu
    var = jnp.mean(xc * xc, axis=-1, keepdims=True)
    return xc * lax.rsqrt(var + LN_EPS) * g + b


def _head_masks(width=GROUP_W):
    lane = lax.broadcasted_iota(jnp.int32, (1, width), 1)
    return [(lane // HEAD) == h for h in range(N_H)]


def _head_ones():
    rr = lax.broadcasted_iota(jnp.int32, (GROUP_W, 1), 0) // HEAD
    cc = lax.broadcasted_iota(jnp.int32, (1, GROUP_W), 1) // HEAD
    same = rr == cc
    return same, jnp.where(same, 1.0, 0.0).astype(BF16)


def _head_sum(x, ones_b):
    return _dot(x.astype(BF16), ones_b)


def _head_norm(y, g, b, ones_b):
    mu = _head_sum(y, ones_b) * (1.0 / HEAD)
    yc = y - mu
    var = _head_sum(yc * yc, ones_b) * (1.0 / HEAD)
    return yc * lax.rsqrt(var + GN_EPS) * g + b


def _col(x, j):
    lane = lax.broadcasted_iota(jnp.int32, (1, x.shape[1]), 1)
    return jnp.sum(jnp.where(lane == j, x, 0.0), axis=1, keepdims=True)


def _valid_rows(t):
    row = t * TB + lax.broadcasted_iota(jnp.int32, (TB, 1), 0)
    return row, row >= (TB - N_META)


def _embed_ln_kernel(x_ref, meta_ref, g_ref, b_ref, o_ref):
    t = pl.program_id(1)

    @pl.when(t == 0)
    def _():
        o_ref[0, 0:TB - N_META, :] = jnp.zeros((TB - N_META, D_MODEL), F32)
        o_ref[0, TB - N_META:TB, :] = _layer_norm(meta_ref[...], g_ref[...], b_ref[...])

    @pl.when(t > 0)
    def _():
        o_ref[0] = _layer_norm(x_ref[0], g_ref[...], b_ref[...])


_SCAN_PAD = TB // 2
_POOL_OFF = SUBLANE + POOL_MAX_W
_INIT, _BODY = "init", "body"
_DONE = object()
_LEAD_STAGES = 5
_RWKV_STAGES, _MLSTM_STAGES, _POOL_LRU_STAGES, _INPROJ_STAGES, _POST_STAGES = 22, 16, 17, 11, 38


def _pool_lru_kernel(phase, pp_ref, plru_ref, poolw_ref, pscale_ref, convw_ref, convb_ref, wax_ref, bax_ref,
                     lam_ref, ypool_ref, ylru_ref, pext, pw2, pw4, pw8, cext, gbuf, s_a, s_u, hcar):
    t = pl.program_id(1)
    if phase == _INIT:
        @pl.when(t == 0)
        def _():
            pext[0:_POOL_OFF, :] = jnp.zeros((_POOL_OFF, GROUP_W), F32)
            for buf in (pw2, pw4, pw8):
                buf[0:SUBLANE, :] = jnp.zeros((SUBLANE, GROUP_W), F32)
            cext[0:SUBLANE, :] = jnp.zeros((SUBLANE, GROUP_W), F32)
            hcar[...] = jnp.zeros((1, GROUP_W), F32)
            s_a[0:_SCAN_PAD, :] = jnp.ones((_SCAN_PAD, GROUP_W), F32)
            s_u[0:_SCAN_PAD, :] = jnp.zeros((_SCAN_PAD, GROUP_W), F32)
        return

    row, valid = _valid_rows(t)

    u = jnp.where(valid, pp_ref[...], 0.0)
    lo_p, hi_p = SUBLANE, _POOL_OFF + TB
    pext[_POOL_OFF:hi_p, :] = u
    cext[SUBLANE:SUBLANE + TB, :] = jnp.where(valid, plru_ref[:, 0:GROUP_W], 0.0)
    gbuf[...] = plru_ref[:, GROUP_W:2 * GROUP_W]
    pw2[lo_p:hi_p, :] = pext[lo_p:hi_p, :] + pext[lo_p - 1:hi_p - 1, :]
    yield
    pw4[lo_p:hi_p, :] = pw2[lo_p:hi_p, :] + pw2[lo_p - 2:hi_p - 2, :]
    yield
    pw8[lo_p:hi_p, :] = pw4[lo_p:hi_p, :] + pw4[lo_p - 4:hi_p - 4, :]
    yield
    w16 = pw8[_POOL_OFF:hi_p, :] + pw8[_POOL_OFF - 8:hi_p - 8, :]
    grp = lax.broadcasted_iota(jnp.int32, (1, GROUP_W), 1) // HEAD
    acc = jnp.where(grp == 0, pw2[_POOL_OFF:hi_p, :],
                    jnp.where(grp == 1, pw4[_POOL_OFF:hi_p, :],
                              jnp.where(grp == 2, pw8[_POOL_OFF:hi_p, :], w16)))
    win = jnp.left_shift(2, grp)
    pos1 = (row - (TB - N_META) + 1).astype(F32)
    cnt = jnp.clip(pos1, 1.0, win.astype(F32))
    dlt = acc / cnt - u
    ypool_ref[...] = (_bdot(dlt, poolw_ref[...]) * pscale_ref[...]).astype(ypool_ref.dtype)
    pext[lo_p:_POOL_OFF, :] = pext[TB + lo_p:TB + _POOL_OFF, :]
    yield

    xc = jnp.zeros((TB, GROUP_W), F32) + convb_ref[...]
    for j in range(CONV_W):
        off = SUBLANE - (CONV_W - 1) + j
        xc = xc + convw_ref[j:j + 1, :] * cext[off:off + TB, :]
    cext[0:SUBLANE, :] = cext[TB:TB + SUBLANE, :]
    yield
    z = _bdot(xc, wax_ref[...]) + bax_ref[...]
    r = _sigmoid(z[:, 0:GROUP_W])
    i = _sigmoid(z[:, GROUP_W:2 * GROUP_W])
    yield
    log_a = (-LRU_C) * r * _softplus_small(-lam_ref[...])
    a = jnp.exp(log_a)
    uu = jnp.where(valid, jnp.sqrt(-jnp.tanh(log_a) * (a * a + 1.0)) * (i * xc), 0.0)
    lo, hi = _SCAN_PAD, _SCAN_PAD + TB
    s_a[lo:hi, :] = a
    s_u[lo:hi, :] = uu
    yield
    d = 1
    while d < TB:
        a0, u0 = s_a[lo:hi, :], s_u[lo:hi, :]
        a1, u1 = s_a[lo - d:hi - d, :], s_u[lo - d:hi - d, :]
        s_u[lo:hi, :] = a0 * u1 + u0
        s_a[lo:hi, :] = a0 * a1
        d *= 2
        yield
    h = s_u[lo:hi, :] + s_a[lo:hi, :] * hcar[...]
    hcar[...] = h[TB - 1:TB, :]
    yield
    gate = gbuf[...]
    gcube = gate * gate * gate
    gelu = 0.5 * gate * (1.0 + jnp.tanh(math.sqrt(2.0 / math.pi) * (gate + 0.044715 * gcube)))
    ylru_ref[...] = (h * gelu).astype(ylru_ref.dtype)


def _mlstm_kernel(phase, p_ref, ifb_ref, gng_ref, gnb_ref, y_ref, c_st, n_st, m_st):
    t = pl.program_id(1)
    if phase == _INIT:
        @pl.when(t == 0)
        def _():
            c_st[...] = jnp.zeros((GROUP_W, GROUP_W), F32)
            n_st[...] = jnp.zeros((1, GROUP_W), F32)
            m_st[...] = jnp.zeros((1, LANE), F32)
        return

    _, valid = _valid_rows(t)
    hms = _head_masks()
    q = jnp.where(valid, p_ref[:, 0:256], 0.0) * (HEAD ** -0.5)
    k = jnp.where(valid, p_ref[:, 256:512], 0.0)
    v = jnp.where(valid, p_ref[:, 512:768], 0.0)
    o = p_ref[:, 768:1024]
    g = p_ref[:, 1024:ML_COLS_PAD] + ifb_ref[...]
    lane = lax.broadcasted_iota(jnp.int32, (1, LANE), 1)
    is_i = lane < N_H
    is_f = jnp.logical_and(lane >= N_H, lane < 2 * N_H)
    logf = jnp.where(jnp.logical_and(is_f, valid), -_softplus(-g), 0.0)
    logi = jnp.where(is_i, jnp.where(valid, g, NEG), 0.0)

    tcol = lax.broadcasted_iota(jnp.int32, (TB, 1), 0)
    srow = lax.broadcasted_iota(jnp.int32, (1, TB), 1)
    causal = srow <= tcol
    tri = jnp.where(causal, 1.0, 0.0).astype(BF16)
    bcum = _dot_exact_lhs(tri, logf)
    srow8 = lax.broadcasted_iota(jnp.int32, (SUBLANE, LANE), 0)
    lane8 = lax.broadcasted_iota(jnp.int32, (SUBLANE, LANE), 1)
    psel = (jnp.where(lane8 == srow8, 1.0, 0.0) - jnp.where(lane8 == srow8 + N_H, 1.0, 0.0)).astype(BF16)
    yield
    zt = _dot_exact_lhs(psel, logi + bcum, _NT)
    yield

    m_prev = m_st[...]
    m_inter_t = bcum + m_prev
    kb, vb = k.astype(BF16), v.astype(BF16)
    qc = _bdot(q, c_st[...])
    qn = q * n_st[...]

    num = jnp.zeros((TB, GROUP_W), F32)
    wk_e = jnp.zeros((TB, GROUP_W), F32)
    sc_row = jnp.zeros((1, GROUP_W), F32)
    m_new_t = jnp.zeros((1, LANE), F32)
    for h in range(N_H):
        hm = hms[h]
        b_h = _col(bcum, N_H + h)
        dmat = jnp.where(causal, b_h + zt[h:h + 1, :], NEG)
        m_intra = jnp.max(dmat, axis=1, keepdims=True)
        m_inter = _col(m_inter_t, N_H + h)
        m_t = jnp.maximum(m_intra, m_inter)
        sc = _dot(jnp.where(hm, q, 0.0).astype(BF16), kb, _NT)
        pm = sc * jnp.exp(dmat - m_t)
        yield
        num_h = _dot(pm.astype(BF16), jnp.where(hm, v, 0.0).astype(BF16))
        w_int = jnp.exp(m_inter - m_t)
        den = (jnp.sum(pm, axis=1, keepdims=True)
               + w_int * jnp.sum(jnp.where(hm, qn, 0.0), axis=1, keepdims=True))
        denom = jnp.maximum(jnp.abs(den), jnp.exp(-m_t))
        num = num + (num_h + w_int * jnp.where(hm, qc, 0.0)) * (1.0 / denom)
        yield
        b_last = b_h[TB - 1:TB, :]
        g_loc = b_last + (_col(logi, h) - b_h)
        m_loc = jnp.max(g_loc, axis=0, keepdims=True)
        m_old = jnp.sum(jnp.where(lane == N_H + h, m_prev, 0.0), axis=1, keepdims=True)
        m_new = jnp.maximum(b_last + m_old, m_loc)
        wk_e = wk_e + jnp.where(hm, jnp.exp(g_loc - m_new), 0.0)
        sc_row = sc_row + jnp.where(hm, jnp.exp(b_last + m_old - m_new), 0.0)
        m_new_t = m_new_t + jnp.where(lane == N_H + h, m_new, 0.0)
        yield

    kw = k * wk_e
    blockdiag, ones_b = _head_ones()
    c_st[...] = sc_row * c_st[...] + jnp.where(blockdiag, _dot(kw.astype(BF16), vb, _TN), 0.0)
    n_st[...] = sc_row * n_st[...] + jnp.sum(kw, axis=0, keepdims=True)
    m_st[...] = m_new_t
    yield

    y_ref[...] = (_head_norm(num, gng_ref[...], gnb_ref[...], ones_b) * _sigmoid(o)).astype(y_ref.dtype)


def _rwkv_kernel(has_vfirst, phase, *refs):
    if has_vfirst:
        (p_ref, vf_ref, mu_ref, w0_ref, wup_ref, a0_ref, aup_ref, gup_ref, kk_ref, ka_ref, rk_ref, gng_ref,
         gnb_ref, v0_ref, vdn_ref, vup_ref, y_ref, pext, s_st) = refs
    else:
        (p_ref, mu_ref, w0_ref, wup_ref, a0_ref, aup_ref, gup_ref, kk_ref, ka_ref, rk_ref, gng_ref,
         gnb_ref, y_ref, vfo_ref, pext, s_st) = refs
    t = pl.program_id(1)
    if phase == _INIT:
        @pl.when(t == 0)
        def _():
            pext[0:SUBLANE, :] = jnp.zeros((SUBLANE, RW_COLS), F32)
            s_st[...] = jnp.zeros((GROUP_W, GROUP_W), F32)
        return

    _, valid = _valid_rows(t)
    blockdiag, bd_b = _head_ones()
    p = jnp.where(valid, p_ref[...], 0.0)
    pext[SUBLANE:SUBLANE + TB, :] = p
    prev = pext[SUBLANE - 1:SUBLANE - 1 + TB, :]
    pext[0:SUBLANE, :] = pext[TB:TB + SUBLANE, :]
    pm = p + (prev - p) * mu_ref[...]
    yield
    r = pm[:, 0:256]
    k = pm[:, 256:512]
    v = pm[:, 512:768]
    lora = pm[:, 768:RW_COLS]
    ld = (-math.exp(-0.5)) * _sigmoid(w0_ref[...] + _bdot(jnp.tanh(lora), wup_ref[...]))
    a = _sigmoid(a0_ref[...] + _bdot(lora, aup_ref[...]))
    g = _bdot(_sigmoid(lora), gup_ref[...])
    yield
    if has_vfirst:
        vmix = _sigmoid(v0_ref[...] + _bdot(_bdot(v, vdn_ref[...]), vup_ref[...]))
        v = v + (vf_ref[...] - v) * vmix
    else:
        vfo_ref[...] = v
    kk = k * kk_ref[...]
    kk = kk * lax.rsqrt(_head_sum(kk * kk, bd_b) + 1e-12)
    k = k * (1.0 + (a - 1.0) * ka_ref[...])
    bvec = kk * a
    yield

    tcol = lax.broadcasted_iota(jnp.int32, (TB, 1), 0)
    srow = lax.broadcasted_iota(jnp.int32, (1, TB), 1)
    cum_mask = jnp.logical_and((tcol // RW_CHUNK) == (srow // RW_CHUNK), srow <= tcol)
    lc = _dot_exact_lhs(jnp.where(cum_mask, 1.0, 0.0).astype(BF16), ld)
    n_chunks = TB // RW_CHUNK
    chunks = range(n_chunks)
    lc_last = [lc[(c + 1) * RW_CHUNK - 1:(c + 1) * RW_CHUNK, :] for c in chunks]
    lc_end = jnp.concatenate([jnp.broadcast_to(x, (RW_CHUNK, GROUP_W)) for x in lc_last], axis=0)
    yield
    e_neg = jnp.exp(-lc)
    kap = (kk * jnp.exp(lc - ld)).astype(BF16)
    kt = (k * e_neg).astype(BF16)
    bt = (bvec * e_neg).astype(BF16)
    rt = r * jnp.exp(lc)
    rtb = rt.astype(BF16)
    vb = v.astype(BF16)
    e_end = jnp.exp(lc_end - lc)
    kbar = (k * e_end).astype(BF16)
    bbar = (bvec * e_end).astype(BF16)
    yield

    def bd(xb):
        return jnp.concatenate([xb] * N_H, axis=0) * bd_b

    tloc = lax.broadcasted_iota(jnp.int32, (RW_CHUNK, 1), 0)
    sloc = lax.broadcasted_iota(jnp.int32, (1, GROUP_W), 1) % RW_CHUNK
    strict = sloc < tloc
    incl = sloc <= tloc
    eye_pk = jnp.where(sloc == tloc, 1.0, 0.0)
    sls = [slice(c * RW_CHUNK, (c + 1) * RW_CHUNK) for c in chunks]

    lhs_kr = [jnp.concatenate([kap[sl], rtb[sl]], axis=0) for sl in sls]
    s_k = [_dot(lhs_kr[c], bd(kt[sls[c]]), _NT) for c in chunks]
    s_b = [_dot(lhs_kr[c], bd(bt[sls[c]]), _NT) for c in chunks]
    yield
    nmat = [jnp.where(strict, -s_b[c][0:RW_CHUNK], 0.0) for c in chunks]
    zs = [eye_pk + nmat[c] for c in chunks]
    ypow = [nmat[c].astype(BF16) for c in chunks]
    ypow = [_dot(ypow[c], bd(ypow[c])).astype(BF16) for c in chunks]
    n_iter = int(math.log2(RW_CHUNK)) - 1
    yield
    for it in range(n_iter):
        rhs = [bd(ypow[c]) for c in chunks]
        if it + 1 < n_iter:
            res = [_dot(jnp.concatenate([ypow[c], zs[c].astype(BF16)], axis=0), rhs[c]) for c in chunks]
            ypow = [res[c][0:RW_CHUNK].astype(BF16) for c in chunks]
            zs = [zs[c] + res[c][RW_CHUNK:2 * RW_CHUNK] for c in chunks]
        else:
            zs = [zs[c] + _dot(zs[c].astype(BF16), rhs[c]) for c in chunks]
        yield
    tinv = [zs[c].astype(BF16) for c in chunks]
    lhs_v = [jnp.concatenate([jnp.where(strict, s_k[c][0:RW_CHUNK], 0.0),
                              jnp.where(incl, s_k[c][RW_CHUNK:2 * RW_CHUNK], 0.0)], axis=0).astype(BF16)
             for c in chunks]
    res_v = [_dot(lhs_v[c], bd(vb[sls[c]])) for c in chunks]
    yield
    p1 = [_dot(tinv[c], bd(kap[sls[c]])) for c in chunks]
    p2 = [_dot(tinv[c], bd(res_v[c][0:RW_CHUNK].astype(BF16))) for c in chunks]
    yield
    a_rb = [jnp.where(incl, s_b[c][RW_CHUNK:2 * RW_CHUNK], 0.0).astype(BF16) for c in chunks]
    y0 = [res_v[c][RW_CHUNK:2 * RW_CHUNK] - _dot(a_rb[c], bd(p2[c].astype(BF16))) for c in chunks]
    qq = [rt[sls[c]] - _dot(a_rb[c], bd(p1[c].astype(BF16))) for c in chunks]
    yield

    s_val = s_st[...]
    ys = []
    for c in chunks:
        sl = sls[c]
        lhs = jnp.concatenate([p1[c], qq[c]], axis=0)
        res = _bdot(lhs, s_val, _NT)
        u_c = res[0:RW_CHUNK] + p2[c]
        ys.append(y0[c] + res[RW_CHUNK:2 * RW_CHUNK])
        lt = jnp.concatenate([vb[sl], (-u_c).astype(BF16)], axis=0)
        rt2 = jnp.concatenate([kbar[sl], bbar[sl]], axis=0)
        s_val = s_val * jnp.exp(lc_last[c]) + jnp.where(blockdiag, _dot(lt, rt2, _TN), 0.0)
        yield
    s_st[...] = s_val
    y = jnp.concatenate(ys, axis=0)

    yn = _head_norm(y, gng_ref[...], gnb_ref[...], bd_b)
    yield
    bonus = _head_sum(r * k * rk_ref[...], bd_b) * v
    y_ref[...] = ((yn + bonus) * g).astype(y_ref.dtype)


_P_SEGMENTS = (("pool", 0, 256), ("rw", 256, 1152), ("lru", 1152, 1664), ("ml", 1664, D_IN_PAD))


def _inproj_stages(h_ref, w_ref, p_refs):
    hb = h_ref[...].astype(BF16)
    for name, lo, hi in _P_SEGMENTS:
        for c0 in range(lo, hi, GROUP_W):
            c1 = min(c0 + GROUP_W, hi)
            p_refs[name][:, c0 - lo:c1 - lo] = _dot(hb, w_ref[:, c0:c1])
            yield


def _post_stages(alpha, y_s, h_ref, wo_ref, g1_ref, b1_ref, w1_ref, w2_ref, g2_ref, b2_ref, o_ref,
                 h1_s, hb_s, zb_s, acc_s):
    slabs = [slice(n * GROUP_W, (n + 1) * GROUP_W) for n in range(D_MODEL // GROUP_W)]
    for ns in slabs:
        h1_s[:, ns] = alpha * h_ref[:, ns] + _dot(y_s[...], wo_ref[:, ns])
        yield
    h1 = _layer_norm(h1_s[...], g1_ref[...], b1_ref[...])
    h1_s[...] = h1
    hb_s[...] = h1.astype(BF16)
    yield
    for j in range(D_FF // D_MODEL):
        for ns in slabs:
            z = jnp.maximum(_dot(hb_s[...], w1_ref[:, j * D_MODEL + ns.start:j * D_MODEL + ns.stop]), 0.0)
            zb_s[:, ns] = (z * z).astype(BF16)
            yield
        for ns in slabs:
            upd = _dot(zb_s[...], w2_ref[j * D_MODEL:(j + 1) * D_MODEL, ns])
            acc_s[:, ns] = upd if j == 0 else acc_s[:, ns] + upd
            yield
    o_ref[...] = _layer_norm(alpha * h1_s[...] + acc_s[...], g2_ref[...], b2_ref[...])


def _fused_layer_kernel(parts, alpha, *refs):
    h_next_ref, h_first_ref, h_prev_ref, w_ref = refs[:4]
    post_consts = refs[4:11]
    refs = refs[11:]
    n_in = sum(p[3] + p[4] for p in parts)
    n_out = 1 + sum(p[6] for p in parts)
    ins, outs, scr = refs[:n_in], refs[n_in:n_in + n_out], refs[n_in + n_out:]
    p_refs = {name: r for (name, _, _), r in zip(_P_SEGMENTS, scr)}
    y_s = scr[len(_P_SEGMENTS)]
    post_scr = scr[len(_P_SEGMENTS) + 1:len(_P_SEGMENTS) + 5]
    scr = scr[len(_P_SEGMENTS) + 5:]
    h_out_ref, outs = outs[0], outs[1:]

    @pl.when(pl.program_id(1) == 0)
    def _():
        y_s[...] = jnp.zeros(y_s.shape, y_s.dtype)
        for _ in _inproj_stages(h_first_ref.at[0], w_ref, p_refs):
            pass

    for phase in (_INIT, _BODY):
        gens, totals = [], []
        i = o = s = 0
        for fn, n_stages, pnames, nt, nc, y_slots, no, ns in parts:
            args = [p_refs[n] for n in pnames] + [r.at[0] for r in ins[i:i + nt]]
            args += list(ins[i + nt:i + nt + nc])
            args += [y_s.at[:, k * GROUP_W:(k + 1) * GROUP_W] for k in y_slots]
            args += [r.at[0] for r in outs[o:o + no]] + list(scr[s:s + ns])
            gens.append(fn(phase, *args))
            totals.append(n_stages)
            i, o, s = i + nt + nc, o + no, s + ns
        if phase == _BODY:
            gens.append(_inproj_stages(h_next_ref.at[0], w_ref, p_refs))
            totals.append(_INPROJ_STAGES)
            gens.append(_post_stages(alpha, y_s, h_prev_ref.at[0], *post_consts, h_out_ref.at[0], *post_scr))
            totals.append(_POST_STAGES)
        done = [0] * len(gens)

        def advance(j):
            if next(gens[j], _DONE) is _DONE:
                done[j] = None
            else:
                done[j] += 1

        for _ in range(_LEAD_STAGES):
            advance(0)
        for j in range(len(gens)):
            if done[j] == 0:
                advance(j)
        while any(d is not None for d in done):
            j = min((d / totals[k], k) for k, d in enumerate(done) if d is not None)[1]
            advance(j)


def _layer_call(name, batch, tp, alpha, h, w_in_b, post_consts, parts):
    n_t = tp // TB

    def tspec(width, index):
        return pl.BlockSpec((1, TB, width), index)

    def cspec(shape):
        return pl.BlockSpec(shape, lambda b, t: (0,) * len(shape), pipeline_mode=pl.Buffered(1))

    in_specs = [tspec(D_MODEL, lambda b, t: (b, jnp.minimum(t + 1, n_t - 1), 0)),
                tspec(D_MODEL, lambda b, t: (b, 0, 0)),
                tspec(D_MODEL, lambda b, t: (b, jnp.maximum(t - 1, 0), 0)),
                cspec(w_in_b.shape)] + [cspec(a.shape) for a in post_consts]
    operands = [h, h, h, w_in_b] + list(post_consts)
    out_shape = [jax.ShapeDtypeStruct((batch, tp, D_MODEL), F32)]
    out_specs = [tspec(D_MODEL, lambda b, t: (b, jnp.maximum(t - 1, 0), 0))]
    scratch = [pltpu.VMEM((TB, hi - lo), F32) for _, lo, hi in _P_SEGMENTS]
    scratch += [pltpu.VMEM((TB, D_MODEL), BF16), pltpu.VMEM((TB, D_MODEL), F32), pltpu.VMEM((TB, D_MODEL), BF16),
                pltpu.VMEM((TB, D_MODEL), BF16), pltpu.VMEM((TB, D_MODEL), F32)]
    sig = []
    for fn, n_stages, pnames, time_ins, const_ins, y_slots, extra_widths, scr in parts:
        in_specs += [tspec(a.shape[2], lambda b, t: (b, jnp.minimum(t, n_t - 1), 0)) for a in time_ins]
        in_specs += [cspec(a.shape) for a in const_ins]
        operands += list(time_ins) + list(const_ins)
        out_shape += [jax.ShapeDtypeStruct((batch, tp + TB, w), F32) for w in extra_widths]
        out_specs += [tspec(w, lambda b, t: (b, t, 0)) for w in extra_widths]
        scratch += [pltpu.VMEM(tuple(shape), F32) for shape in scr]
        sig.append((fn, n_stages, tuple(pnames), len(time_ins), len(const_ins), tuple(y_slots),
                    len(extra_widths), len(scr)))
    return pl.pallas_call(
        functools.partial(_fused_layer_kernel, tuple(sig), alpha),
        grid=(batch, n_t + 1),
        in_specs=in_specs,
        out_specs=out_specs,
        out_shape=out_shape,
        scratch_shapes=scratch,
        compiler_params=pltpu.CompilerParams(dimension_semantics=("arbitrary", "arbitrary"),
                                             vmem_limit_bytes=VMEM_LIMIT),
        name=name,
    )(*operands)


def _block_diag(w):
    eye = jnp.eye(N_H, dtype=w.dtype)
    return jnp.einsum('gcd,gh->gchd', w, eye).reshape(GROUP_W, GROUP_W)


def _pad_rows(w, total, offset):
    return jnp.zeros((total, w.shape[1]), w.dtype).at[offset:offset + w.shape[0]].set(w)


def _row(v):
    return v.reshape(1, -1)


@jax.jit
def kernel(x, meta, emb_ln_g, emb_ln_b, w_in, w_out, pool_w, pool_scale, rw_mu, rw_w0, rw_w_up, rw_a0, rw_a_up, rw_g_up, rw_k_k, rw_k_a, rw_r_k, rw_gn_g, rw_gn_b, rw_v0, rw_v_down, rw_v_up, lru_conv_w, lru_conv_b, lru_ga_w, lru_ga_b, lru_gx_w, lru_gx_b, lru_lambda, ml_if_b, ml_gn_g, ml_gn_b, ln1_g, ln1_b, ln2_g, ln2_b, mlp_w1, mlp_w2):
    batch, seq, _ = x.shape
    depth = w_in.shape[0]
    assert seq % TB == 0 and x.shape[2] == D_MODEL
    tp = TB + seq
    alpha = (2 * depth) ** 0.25

    h = pl.pallas_call(
        _embed_ln_kernel,
        grid=(batch, tp // TB),
        in_specs=[pl.BlockSpec((1, TB, D_MODEL), lambda b, t: (b, jnp.maximum(t - 1, 0), 0)),
                  pl.BlockSpec((N_META, D_MODEL), lambda b, t: (0, 0)),
                  pl.BlockSpec((1, D_MODEL), lambda b, t: (0, 0)),
                  pl.BlockSpec((1, D_MODEL), lambda b, t: (0, 0))],
        out_specs=pl.BlockSpec((1, TB, D_MODEL), lambda b, t: (b, t, 0)),
        out_shape=jax.ShapeDtypeStruct((batch, tp, D_MODEL), F32),
        compiler_params=pltpu.CompilerParams(dimension_semantics=("arbitrary", "arbitrary")),
        name="embed_ln",
    )(x, meta.astype(x.dtype), _row(emb_ln_g), _row(emb_ln_b))

    w_in_b = jnp.pad(w_in, ((0, 0), (0, 0), (0, D_IN_PAD - w_in.shape[2]))).astype(BF16)
    w_out_b = w_out.astype(BF16)
    w1_b = mlp_w1.astype(BF16)
    w2_b = mlp_w2.astype(BF16)

    v_first = None
    for l in range(depth):
        wax = jnp.concatenate([_block_diag(lru_ga_w[l]), _block_diag(lru_gx_w[l])], axis=1).astype(BF16)
        bax = jnp.concatenate([lru_ga_b[l], lru_gx_b[l]]).reshape(1, 2 * GROUP_W)
        pool_lru_part = (
            _pool_lru_kernel, _POOL_LRU_STAGES, ("pool", "lru"), [],
            [_block_diag(pool_w[l]).astype(BF16), _row(pool_scale[l]), lru_conv_w[l], _row(lru_conv_b[l]),
             wax, bax, _row(lru_lambda[l])],
            (0, 2), [],
            [(TB + _POOL_OFF, GROUP_W)] * 4 + [(TB + SUBLANE, GROUP_W), (TB, GROUP_W),
                                               (TB + _SCAN_PAD, GROUP_W), (TB + _SCAN_PAD, GROUP_W),
                                               (1, GROUP_W)])

        ifb = jnp.zeros((1, LANE), F32).at[0, 0:2 * N_H].set(ml_if_b[l])
        mlstm_part = (
            _mlstm_kernel, _MLSTM_STAGES, ("ml",), [], [ifb, _row(ml_gn_g[l]), _row(ml_gn_b[l])], (3,), [],
            [(GROUP_W, GROUP_W), (1, GROUP_W), (1, LANE)])

        wup = _pad_rows(rw_w_up[l], LANE, 0).astype(BF16)
        aup = _pad_rows(rw_a_up[l], LANE, 32).astype(BF16)
        gup = _pad_rows(rw_g_up[l], LANE, 64).astype(BF16)
        rw_consts = [_row(rw_mu[l]), _row(rw_w0[l]), wup, _row(rw_a0[l]), aup, gup, _row(rw_k_k[l]),
                     _row(rw_k_a[l]), _row(rw_r_k[l]), _row(rw_gn_g[l]), _row(rw_gn_b[l])]
        rw_scratch = [(TB + SUBLANE, RW_COLS), (GROUP_W, GROUP_W)]
        post_consts = [w_out_b[l], _row(ln1_g[l]), _row(ln1_b[l]), w1_b[l], w2_b[l], _row(ln2_g[l]),
                       _row(ln2_b[l])]
        if l == 0:
            rwkv_part = (functools.partial(_rwkv_kernel, False), _RWKV_STAGES, ("rw",), [], rw_consts,
                         (1,), [GROUP_W], rw_scratch)
            h, v_first = _layer_call("layer_first", batch, tp, alpha, h, w_in_b[l], post_consts,
                                     [rwkv_part, mlstm_part, pool_lru_part])
        else:
            vdn = jnp.pad(rw_v_down[l - 1], ((0, 0), (0, LANE - rw_v_down.shape[2]))).astype(BF16)
            vup = _pad_rows(rw_v_up[l - 1], LANE, 0).astype(BF16)
            rwkv_part = (functools.partial(_rwkv_kernel, True), _RWKV_STAGES, ("rw",), [v_first],
                         rw_consts + [_row(rw_v0[l - 1]), vdn, vup], (1,), [], rw_scratch)
            h, = _layer_call("layer", batch, tp, alpha, h, w_in_b[l], post_consts,
                             [rwkv_part, mlstm_part, pool_lru_part])

    return h[:, TB:, :]
```

```python
import functools
import math

import jax
import jax.numpy as jnp
from jax import lax
from jax.experimental import pallas as pl
from jax.experimental.pallas import tpu as pltpu

D_MODEL = 1024
N_META = 16
GROUP_W = 256
HEAD = 64
N_H = 4
POOL_MAX_W = 16
CONV_W = 4
LRU_C = 8.0
LN_EPS = 1e-5
GN_EPS = 64e-5
NEG = -1e30
D_FF = 4096
RW_COLS = 896
ML_COLS_PAD = 1152
D_IN_PAD = 2816
LANE = 128
SUBLANE = 8

TB = 256
RW_CHUNK = 64
VMEM_LIMIT = 60 * 1024 * 1024

F32 = jnp.float32
BF16 = jnp.bfloat16

_NN = (((1,), (0,)), ((), ()))
_NT = (((1,), (1,)), ((), ()))
_TN = (((0,), (0,)), ((), ()))


def _dot(a, b, dims=_NN):
    return lax.dot_general(a, b, dims, preferred_element_type=F32)


def _bdot(a, b, dims=_NN):
    return _dot(a.astype(BF16), b.astype(BF16), dims)


def _split3(x):
    hi = x.astype(BF16)
    r1 = x - hi.astype(F32)
    mid = r1.astype(BF16)
    lo = (r1 - mid.astype(F32)).astype(BF16)
    return hi, mid, lo


def _dot_exact_lhs(e_bf16, x, dims=_NN):
    hi, mid, lo = _split3(x)
    return _dot(e_bf16, hi, dims) + _dot(e_bf16, mid, dims) + _dot(e_bf16, lo, dims)


def _softplus(x):
    return jnp.maximum(x, 0.0) + jnp.log(1.0 + jnp.exp(-jnp.abs(x)))


def _sigmoid(x):
    return 0.5 * jnp.tanh(0.5 * x) + 0.5


def _softplus_small(x):
    return jnp.maximum(x, 0.0) + jnp.log1p(jnp.exp(-jnp.abs(x)))


def _layer_norm(x, g, b):
    mu = jnp.mean(x, axis=-1, keepdims=True)
    xc = x - mu
    var = jnp.mean(xc * xc, axis=-1, keepdims=True)
    return xc * lax.rsqrt(var + LN_EPS) * g + b


def _head_masks(width=GROUP_W):
    lane = lax.broadcasted_iota(jnp.int32, (1, width), 1)
    return [(lane // HEAD) == h for h in range(N_H)]


def _head_ones():
    rr = lax.broadcasted_iota(jnp.int32, (GROUP_W, 1), 0) // HEAD
    cc = lax.broadcasted_iota(jnp.int32, (1, GROUP_W), 1) // HEAD
    same = rr == cc
    return same, jnp.where(same, 1.0, 0.0).astype(BF16)


def _head_sum(x, ones_b):
    return _dot(x.astype(BF16), ones_b)


def _head_norm(y, g, b, ones_b):
    mu = _head_sum(y, ones_b) * (1.0 / HEAD)
    yc = y - mu
    var = _head_sum(yc * yc, ones_b) * (1.0 / HEAD)
    return yc * lax.rsqrt(var + GN_EPS) * g + b


def _col(x, j):
    lane = lax.broadcasted_iota(jnp.int32, (1, x.shape[1]), 1)
    return jnp.sum(jnp.where(lane == j, x, 0.0), axis=1, keepdims=True)


def _valid_rows(t):
    row = t * TB + lax.broadcasted_iota(jnp.int32, (TB, 1), 0)
    return row, row >= (TB - N_META)


def _embed_ln_kernel(x_ref, meta_ref, g_ref, b_ref, o_ref):
    t = pl.program_id(0)
    batch = o_ref.shape[0]

    @pl.when(t == 0)
    def _():
        meta_n = _layer_norm(meta_ref[...], g_ref[...], b_ref[...])
        for b in range(batch):
            o_ref[b, 0:TB - N_META, :] = jnp.zeros((TB - N_META, D_MODEL), F32)
            o_ref[b, TB - N_META:TB, :] = meta_n

    @pl.when(t > 0)
    def _():
        for b in range(batch):
            o_ref[b] = _layer_norm(x_ref[b], g_ref[...], b_ref[...])


_SCAN_PAD = TB // 2
_POOL_OFF = SUBLANE + POOL_MAX_W
_INIT, _BODY = "init", "body"
_DONE = object()
_LEAD_STAGES = 2
_RWKV_STAGES, _MLSTM_STAGES, _POOL_LRU_STAGES, _INPROJ_STAGES, _POST_STAGES = 22, 16, 17, 11, 38


def _pool_lru_kernel(phase, pp_ref, plru_ref, poolw_ref, pscale_ref, convw_ref, convb_ref, wax_ref, bax_ref,
                     lam_ref, ypool_ref, ylru_ref, pext, pw2, pw4, pw8, cext, gbuf, s_a, s_u, hcar):
    t = pl.program_id(1)
    if phase == _INIT:
        @pl.when(t == 0)
        def _():
            pext[0:_POOL_OFF, :] = jnp.zeros((_POOL_OFF, GROUP_W), F32)
            for buf in (pw2, pw4, pw8):
                buf[0:SUBLANE, :] = jnp.zeros((SUBLANE, GROUP_W), F32)
            cext[0:SUBLANE, :] = jnp.zeros((SUBLANE, GROUP_W), F32)
            hcar[...] = jnp.zeros((1, GROUP_W), F32)
            s_a[0:_SCAN_PAD, :] = jnp.ones((_SCAN_PAD, GROUP_W), F32)
            s_u[0:_SCAN_PAD, :] = jnp.zeros((_SCAN_PAD, GROUP_W), F32)
        return

    row, valid = _valid_rows(t)

    u = jnp.where(valid, pp_ref[...], 0.0)
    lo_p, hi_p = SUBLANE, _POOL_OFF + TB
    pext[_POOL_OFF:hi_p, :] = u
    cext[SUBLANE:SUBLANE + TB, :] = jnp.where(valid, plru_ref[:, 0:GROUP_W], 0.0)
    gbuf[...] = plru_ref[:, GROUP_W:2 * GROUP_W]
    pw2[lo_p:hi_p, :] = pext[lo_p:hi_p, :] + pext[lo_p - 1:hi_p - 1, :]
    yield
    pw4[lo_p:hi_p, :] = pw2[lo_p:hi_p, :] + pw2[lo_p - 2:hi_p - 2, :]
    yield
    pw8[lo_p:hi_p, :] = pw4[lo_p:hi_p, :] + pw4[lo_p - 4:hi_p - 4, :]
    yield
    w16 = pw8[_POOL_OFF:hi_p, :] + pw8[_POOL_OFF - 8:hi_p - 8, :]
    grp = lax.broadcasted_iota(jnp.int32, (1, GROUP_W), 1) // HEAD
    acc = jnp.where(grp == 0, pw2[_POOL_OFF:hi_p, :],
                    jnp.where(grp == 1, pw4[_POOL_OFF:hi_p, :],
                              jnp.where(grp == 2, pw8[_POOL_OFF:hi_p, :], w16)))
    win = jnp.left_shift(2, grp)
    pos1 = (row - (TB - N_META) + 1).astype(F32)
    cnt = jnp.clip(pos1, 1.0, win.astype(F32))
    dlt = acc / cnt - u
    ypool_ref[...] = (_bdot(dlt, poolw_ref[...]) * pscale_ref[...]).astype(ypool_ref.dtype)
    pext[lo_p:_POOL_OFF, :] = pext[TB + lo_p:TB + _POOL_OFF, :]
    yield

    xc = jnp.zeros((TB, GROUP_W), F32) + convb_ref[...]
    for j in range(CONV_W):
        off = SUBLANE - (CONV_W - 1) + j
        xc = xc + convw_ref[j:j + 1, :] * cext[off:off + TB, :]
    cext[0:SUBLANE, :] = cext[TB:TB + SUBLANE, :]
    yield
    z = _bdot(xc, wax_ref[...]) + bax_ref[...]
    r = _sigmoid(z[:, 0:GROUP_W])
    i = _sigmoid(z[:, GROUP_W:2 * GROUP_W])
    yield
    log_a = (-LRU_C) * r * _softplus_small(-lam_ref[...])
    a = jnp.exp(log_a)
    uu = jnp.where(valid, jnp.sqrt(-jnp.tanh(log_a) * (a * a + 1.0)) * (i * xc), 0.0)
    lo, hi = _SCAN_PAD, _SCAN_PAD + TB
    s_a[lo:hi, :] = a
    s_u[lo:hi, :] = uu
    yield
    d = 1
    while d < TB:
        a0, u0 = s_a[lo:hi, :], s_u[lo:hi, :]
        a1, u1 = s_a[lo - d:hi - d, :], s_u[lo - d:hi - d, :]
        s_u[lo:hi, :] = a0 * u1 + u0
        s_a[lo:hi, :] = a0 * a1
        d *= 2
        yield
    h = s_u[lo:hi, :] + s_a[lo:hi, :] * hcar[...]
    hcar[...] = h[TB - 1:TB, :]
    yield
    gate = gbuf[...]
    gcube = gate * gate * gate
    gelu = 0.5 * gate * (1.0 + jnp.tanh(math.sqrt(2.0 / math.pi) * (gate + 0.044715 * gcube)))
    ylru_ref[...] = (h * gelu).astype(ylru_ref.dtype)


def _mlstm_kernel(phase, p_ref, ifb_ref, gng_ref, gnb_ref, y_ref, c_st, n_st, m_st):
    t = pl.program_id(1)
    if phase == _INIT:
        @pl.when(t == 0)
        def _():
            c_st[...] = jnp.zeros((GROUP_W, GROUP_W), F32)
            n_st[...] = jnp.zeros((1, GROUP_W), F32)
            m_st[...] = jnp.zeros((1, LANE), F32)
        return

    _, valid = _valid_rows(t)
    hms = _head_masks()
    q = jnp.where(valid, p_ref[:, 0:256], 0.0) * (HEAD ** -0.5)
    k = jnp.where(valid, p_ref[:, 256:512], 0.0)
    v = jnp.where(valid, p_ref[:, 512:768], 0.0)
    o = p_ref[:, 768:1024]
    g = p_ref[:, 1024:ML_COLS_PAD] + ifb_ref[...]
    lane = lax.broadcasted_iota(jnp.int32, (1, LANE), 1)
    is_i = lane < N_H
    is_f = jnp.logical_and(lane >= N_H, lane < 2 * N_H)
    logf = jnp.where(jnp.logical_and(is_f, valid), -_softplus(-g), 0.0)
    logi = jnp.where(is_i, jnp.where(valid, g, NEG), 0.0)

    tcol = lax.broadcasted_iota(jnp.int32, (TB, 1), 0)
    srow = lax.broadcasted_iota(jnp.int32, (1, TB), 1)
    causal = srow <= tcol
    tri = jnp.where(causal, 1.0, 0.0).astype(BF16)
    bcum = _dot_exact_lhs(tri, logf)
    srow8 = lax.broadcasted_iota(jnp.int32, (SUBLANE, LANE), 0)
    lane8 = lax.broadcasted_iota(jnp.int32, (SUBLANE, LANE), 1)
    psel = (jnp.where(lane8 == srow8, 1.0, 0.0) - jnp.where(lane8 == srow8 + N_H, 1.0, 0.0)).astype(BF16)
    yield
    zt = _dot_exact_lhs(psel, logi + bcum, _NT)
    yield

    m_prev = m_st[...]
    m_inter_t = bcum + m_prev
    kb, vb = k.astype(BF16), v.astype(BF16)
    qc = _bdot(q, c_st[...])
    qn = q * n_st[...]

    num = jnp.zeros((TB, GROUP_W), F32)
    wk_e = jnp.zeros((TB, GROUP_W), F32)
    sc_row = jnp.zeros((1, GROUP_W), F32)
    m_new_t = jnp.zeros((1, LANE), F32)
    for h in range(N_H):
        hm = hms[h]
        b_h = _col(bcum, N_H + h)
        dmat = jnp.where(causal, b_h + zt[h:h + 1, :], NEG)
        m_intra = jnp.max(dmat, axis=1, keepdims=True)
        m_inter = _col(m_inter_t, N_H + h)
        m_t = jnp.maximum(m_intra, m_inter)
        sc = _dot(jnp.where(hm, q, 0.0).astype(BF16), kb, _NT)
        pm = sc * jnp.exp(dmat - m_t)
        yield
        num_h = _dot(pm.astype(BF16), jnp.where(hm, v, 0.0).astype(BF16))
        w_int = jnp.exp(m_inter - m_t)
        den = (jnp.sum(pm, axis=1, keepdims=True)
               + w_int * jnp.sum(jnp.where(hm, qn, 0.0), axis=1, keepdims=True))
        denom = jnp.maximum(jnp.abs(den), jnp.exp(-m_t))
        num = num + (num_h + w_int * jnp.where(hm, qc, 0.0)) * (1.0 / denom)
        yield
        b_last = b_h[TB - 1:TB, :]
        g_loc = b_last + (_col(logi, h) - b_h)
        m_loc = jnp.max(g_loc, axis=0, keepdims=True)
        m_old = jnp.sum(jnp.where(lane == N_H + h, m_prev, 0.0), axis=1, keepdims=True)
        m_new = jnp.maximum(b_last + m_old, m_loc)
        wk_e = wk_e + jnp.where(hm, jnp.exp(g_loc - m_new), 0.0)
        sc_row = sc_row + jnp.where(hm, jnp.exp(b_last + m_old - m_new), 0.0)
        m_new_t = m_new_t + jnp.where(lane == N_H + h, m_new, 0.0)
        yield

    kw = k * wk_e
    blockdiag, ones_b = _head_ones()
    c_st[...] = sc_row * c_st[...] + jnp.where(blockdiag, _dot(kw.astype(BF16), vb, _TN), 0.0)
    n_st[...] = sc_row * n_st[...] + jnp.sum(kw, axis=0, keepdims=True)
    m_st[...] = m_new_t
    yield

    y_ref[...] = (_head_norm(num, gng_ref[...], gnb_ref[...], ones_b) * _sigmoid(o)).astype(y_ref.dtype)


def _rwkv_kernel(has_vfirst, phase, *refs):
    if has_vfirst:
        (p_ref, vf_ref, mu_ref, w0_ref, wup_ref, a0_ref, aup_ref, gup_ref, kk_ref, ka_ref, rk_ref, gng_ref,
         gnb_ref, v0_ref, vdn_ref, vup_ref, y_ref, pext, s_st) = refs
    else:
        (p_ref, mu_ref, w0_ref, wup_ref, a0_ref, aup_ref, gup_ref, kk_ref, ka_ref, rk_ref, gng_ref,
         gnb_ref, y_ref, vfo_ref, pext, s_st) = refs
    t = pl.program_id(1)
    if phase == _INIT:
        @pl.when(t == 0)
        def _():
            pext[0:SUBLANE, :] = jnp.zeros((SUBLANE, RW_COLS), F32)
            s_st[...] = jnp.zeros((GROUP_W, GROUP_W), F32)
        return

    _, valid = _valid_rows(t)
    blockdiag, bd_b = _head_ones()
    p = jnp.where(valid, p_ref[...], 0.0)
    pext[SUBLANE:SUBLANE + TB, :] = p
    prev = pext[SUBLANE - 1:SUBLANE - 1 + TB, :]
    pext[0:SUBLANE, :] = pext[TB:TB + SUBLANE, :]
    pm = p + (prev - p) * mu_ref[...]
    yield
    r = pm[:, 0:256]
    k = pm[:, 256:512]
    v = pm[:, 512:768]
    lora = pm[:, 768:RW_COLS]
    ld = (-math.exp(-0.5)) * _sigmoid(w0_ref[...] + _bdot(jnp.tanh(lora), wup_ref[...]))
    a = _sigmoid(a0_ref[...] + _bdot(lora, aup_ref[...]))
    g = _bdot(_sigmoid(lora), gup_ref[...])
    yield
    if has_vfirst:
        vmix = _sigmoid(v0_ref[...] + _bdot(_bdot(v, vdn_ref[...]), vup_ref[...]))
        v = v + (vf_ref[...] - v) * vmix
    else:
        vfo_ref[...] = v
    kk = k * kk_ref[...]
    kk = kk * lax.rsqrt(_head_sum(kk * kk, bd_b) + 1e-12)
    k = k * (1.0 + (a - 1.0) * ka_ref[...])
    bvec = kk * a
    yield

    tcol = lax.broadcasted_iota(jnp.int32, (TB, 1), 0)
    srow = lax.broadcasted_iota(jnp.int32, (1, TB), 1)
    cum_mask = jnp.logical_and((tcol // RW_CHUNK) == (srow // RW_CHUNK), srow <= tcol)
    lc = _dot_exact_lhs(jnp.where(cum_mask, 1.0, 0.0).astype(BF16), ld)
    n_chunks = TB // RW_CHUNK
    chunks = range(n_chunks)
    lc_last = [lc[(c + 1) * RW_CHUNK - 1:(c + 1) * RW_CHUNK, :] for c in chunks]
    lc_end = jnp.concatenate([jnp.broadcast_to(x, (RW_CHUNK, GROUP_W)) for x in lc_last], axis=0)
    yield
    e_neg = jnp.exp(-lc)
    kap = (kk * jnp.exp(lc - ld)).astype(BF16)
    kt = (k * e_neg).astype(BF16)
    bt = (bvec * e_neg).astype(BF16)
    rt = r * jnp.exp(lc)
    rtb = rt.astype(BF16)
    vb = v.astype(BF16)
    e_end = jnp.exp(lc_end - lc)
    kbar = (k * e_end).astype(BF16)
    bbar = (bvec * e_end).astype(BF16)
    yield

    def bd(xb):
        return jnp.concatenate([xb] * N_H, axis=0) * bd_b

    tloc = lax.broadcasted_iota(jnp.int32, (RW_CHUNK, 1), 0)
    sloc = lax.broadcasted_iota(jnp.int32, (1, GROUP_W), 1) % RW_CHUNK
    strict = sloc < tloc
    incl = sloc <= tloc
    eye_pk = jnp.where(sloc == tloc, 1.0, 0.0)
    sls = [slice(c * RW_CHUNK, (c + 1) * RW_CHUNK) for c in chunks]

    lhs_kr = [jnp.concatenate([kap[sl], rtb[sl]], axis=0) for sl in sls]
    s_k = [_dot(lhs_kr[c], bd(kt[sls[c]]), _NT) for c in chunks]
    s_b = [_dot(lhs_kr[c], bd(bt[sls[c]]), _NT) for c in chunks]
    yield
    nmat = [jnp.where(strict, -s_b[c][0:RW_CHUNK], 0.0) for c in chunks]
    zs = [eye_pk + nmat[c] for c in chunks]
    ypow = [nmat[c].astype(BF16) for c in chunks]
    ypow = [_dot(ypow[c], bd(ypow[c])).astype(BF16) for c in chunks]
    n_iter = int(math.log2(RW_CHUNK)) - 1
    yield
    for it in range(n_iter):
        rhs = [bd(ypow[c]) for c in chunks]
        if it + 1 < n_iter:
            res = [_dot(jnp.concatenate([ypow[c], zs[c].astype(BF16)], axis=0), rhs[c]) for c in chunks]
            ypow = [res[c][0:RW_CHUNK].astype(BF16) for c in chunks]
            zs = [zs[c] + res[c][RW_CHUNK:2 * RW_CHUNK] for c in chunks]
        else:
            zs = [zs[c] + _dot(zs[c].astype(BF16), rhs[c]) for c in chunks]
        yield
    tinv = [zs[c].astype(BF16) for c in chunks]
    lhs_v = [jnp.concatenate([jnp.where(strict, s_k[c][0:RW_CHUNK], 0.0),
                              jnp.where(incl, s_k[c][RW_CHUNK:2 * RW_CHUNK], 0.0)], axis=0).astype(BF16)
             for c in chunks]
    res_v = [_dot(lhs_v[c], bd(vb[sls[c]])) for c in chunks]
    yield
    p1 = [_dot(tinv[c], bd(kap[sls[c]])) for c in chunks]
    p2 = [_dot(tinv[c], bd(res_v[c][0:RW_CHUNK].astype(BF16))) for c in chunks]
    yield
    a_rb = [jnp.where(incl, s_b[c][RW_CHUNK:2 * RW_CHUNK], 0.0).astype(BF16) for c in chunks]
    y0 = [res_v[c][RW_CHUNK:2 * RW_CHUNK] - _dot(a_rb[c], bd(p2[c].astype(BF16))) for c in chunks]
    qq = [rt[sls[c]] - _dot(a_rb[c], bd(p1[c].astype(BF16))) for c in chunks]
    yield

    s_val = s_st[...]
    ys = []
    for c in chunks:
        sl = sls[c]
        lhs = jnp.concatenate([p1[c], qq[c]], axis=0)
        res = _bdot(lhs, s_val, _NT)
        u_c = res[0:RW_CHUNK] + p2[c]
        ys.append(y0[c] + res[RW_CHUNK:2 * RW_CHUNK])
        lt = jnp.concatenate([vb[sl], (-u_c).astype(BF16)], axis=0)
        rt2 = jnp.concatenate([kbar[sl], bbar[sl]], axis=0)
        s_val = s_val * jnp.exp(lc_last[c]) + jnp.where(blockdiag, _dot(lt, rt2, _TN), 0.0)
        yield
    s_st[...] = s_val
    y = jnp.concatenate(ys, axis=0)

    yn = _head_norm(y, gng_ref[...], gnb_ref[...], bd_b)
    yield
    bonus = _head_sum(r * k * rk_ref[...], bd_b) * v
    y_ref[...] = ((yn + bonus) * g).astype(y_ref.dtype)


_P_SEGMENTS = (("pool", 0, 256), ("rw", 256, 1152), ("lru", 1152, 1664), ("ml", 1664, D_IN_PAD))


def _inproj_stages(h_ref, w_ref, p_refs):
    hb = h_ref[...].astype(BF16)
    for name, lo, hi in _P_SEGMENTS:
        for c0 in range(lo, hi, GROUP_W):
            c1 = min(c0 + GROUP_W, hi)
            p_refs[name][:, c0 - lo:c1 - lo] = _dot(hb, w_ref[:, c0:c1])
            yield


def _post_stages(alpha, y_s, h_ref, wo_ref, g1_ref, b1_ref, w1_ref, w2_ref, g2_ref, b2_ref, o_ref,
                 h1_s, hb_s, zb_s, acc_s):
    slabs = [slice(n * GROUP_W, (n + 1) * GROUP_W) for n in range(D_MODEL // GROUP_W)]
    for ns in slabs:
        h1_s[:, ns] = alpha * h_ref[:, ns] + _dot(y_s[...], wo_ref[:, ns])
        yield
    h1 = _layer_norm(h1_s[...], g1_ref[...], b1_ref[...])
    h1_s[...] = h1
    hb_s[...] = h1.astype(BF16)
    yield
    for j in range(D_FF // D_MODEL):
        for ns in slabs:
            z = jnp.maximum(_dot(hb_s[...], w1_ref[:, j * D_MODEL + ns.start:j * D_MODEL + ns.stop]), 0.0)
            zb_s[:, ns] = (z * z).astype(BF16)
            yield
        for ns in slabs:
            upd = _dot(zb_s[...], w2_ref[j * D_MODEL:(j + 1) * D_MODEL, ns])
            acc_s[:, ns] = upd if j == 0 else acc_s[:, ns] + upd
            yield
    o_ref[...] = _layer_norm(alpha * h1_s[...] + acc_s[...], g2_ref[...], b2_ref[...])


def _fused_layer_kernel(parts, alpha, *refs):
    h_next_ref, h_first_ref, h_prev_ref, w_ref = refs[:4]
    post_consts = refs[4:11]
    refs = refs[11:]
    n_in = sum(p[3] + p[4] for p in parts)
    n_out = 1 + sum(p[6] for p in parts)
    ins, outs, scr = refs[:n_in], refs[n_in:n_in + n_out], refs[n_in + n_out:]
    p_refs = {name: r for (name, _, _), r in zip(_P_SEGMENTS, scr)}
    y_s = scr[len(_P_SEGMENTS)]
    post_scr = scr[len(_P_SEGMENTS) + 1:len(_P_SEGMENTS) + 5]
    scr = scr[len(_P_SEGMENTS) + 5:]
    h_out_ref, outs = outs[0], outs[1:]

    @pl.when(pl.program_id(1) == 0)
    def _():
        y_s[...] = jnp.zeros(y_s.shape, y_s.dtype)
        for _ in _inproj_stages(h_first_ref.at[0], w_ref, p_refs):
            pass

    for phase in (_INIT, _BODY):
        gens, totals = [], []
        i = o = s = 0
        for fn, n_stages, pnames, nt, nc, y_slots, no, ns in parts:
            args = [p_refs[n] for n in pnames] + [r.at[0] for r in ins[i:i + nt]]
            args += list(ins[i + nt:i + nt + nc])
            args += [y_s.at[:, k * GROUP_W:(k + 1) * GROUP_W] for k in y_slots]
            args += [r.at[0] for r in outs[o:o + no]] + list(scr[s:s + ns])
            gens.append(fn(phase, *args))
            totals.append(n_stages)
            i, o, s = i + nt + nc, o + no, s + ns
        if phase == _BODY:
            gens.append(_inproj_stages(h_next_ref.at[0], w_ref, p_refs))
            totals.append(_INPROJ_STAGES)
            gens.append(_post_stages(alpha, y_s, h_prev_ref.at[0], *post_consts, h_out_ref.at[0], *post_scr))
            totals.append(_POST_STAGES)
        done = [0] * len(gens)

        def advance(j):
            if next(gens[j], _DONE) is _DONE:
                done[j] = None
            else:
                done[j] += 1

        for _ in range(_LEAD_STAGES):
            advance(0)
        for j in range(len(gens)):
            if done[j] == 0:
                advance(j)
        while any(d is not None for d in done):
            j = min((d / totals[k], k) for k, d in enumerate(done) if d is not None)[1]
            advance(j)


def _layer_call(name, batch, tp, alpha, h, w_in_b, post_consts, parts, drop_front):
    n_t = tp // TB
    skip = 1 if drop_front else 0

    def tspec(width, index):
        return pl.BlockSpec((1, TB, width), index)

    def cspec(shape):
        return pl.BlockSpec(shape, lambda b, t: (0,) * len(shape), pipeline_mode=pl.Buffered(1))

    in_specs = [tspec(D_MODEL, lambda b, t: (b, jnp.minimum(t + 1, n_t - 1), 0)),
                tspec(D_MODEL, lambda b, t: (b, 0, 0)),
                tspec(D_MODEL, lambda b, t: (b, jnp.maximum(t - 1, 0), 0)),
                cspec(w_in_b.shape)] + [cspec(a.shape) for a in post_consts]
    operands = [h, h, h, w_in_b] + list(post_consts)
    out_shape = [jax.ShapeDtypeStruct((batch, tp - skip * TB, D_MODEL), F32)]
    out_specs = [tspec(D_MODEL, lambda b, t: (b, jnp.maximum(t - 1 - skip, 0), 0))]
    scratch = [pltpu.VMEM((TB, hi - lo), F32) for _, lo, hi in _P_SEGMENTS]
    scratch += [pltpu.VMEM((TB, D_MODEL), BF16), pltpu.VMEM((TB, D_MODEL), F32), pltpu.VMEM((TB, D_MODEL), BF16),
                pltpu.VMEM((TB, D_MODEL), BF16), pltpu.VMEM((TB, D_MODEL), F32)]
    sig = []
    for fn, n_stages, pnames, time_ins, const_ins, y_slots, extra_widths, scr in parts:
        in_specs += [tspec(a.shape[2], lambda b, t: (b, jnp.minimum(t, n_t - 1), 0)) for a in time_ins]
        in_specs += [cspec(a.shape) for a in const_ins]
        operands += list(time_ins) + list(const_ins)
        out_shape += [jax.ShapeDtypeStruct((batch, tp + TB, w), F32) for w in extra_widths]
        out_specs += [tspec(w, lambda b, t: (b, t, 0)) for w in extra_widths]
        scratch += [pltpu.VMEM(tuple(shape), F32) for shape in scr]
        sig.append((fn, n_stages, tuple(pnames), len(time_ins), len(const_ins), tuple(y_slots),
                    len(extra_widths), len(scr)))
    return pl.pallas_call(
        functools.partial(_fused_layer_kernel, tuple(sig), alpha),
        grid=(batch, n_t + 1),
        in_specs=in_specs,
        out_specs=out_specs,
        out_shape=out_shape,
        scratch_shapes=scratch,
        compiler_params=pltpu.CompilerParams(dimension_semantics=("arbitrary", "arbitrary"),
                                             vmem_limit_bytes=VMEM_LIMIT),
        name=name,
    )(*operands)


def _block_diag(w):
    eye = jnp.eye(N_H, dtype=w.dtype)
    return jnp.einsum('gcd,gh->gchd', w, eye).reshape(GROUP_W, GROUP_W)


def _pad_rows(w, total, offset):
    return jnp.zeros((total, w.shape[1]), w.dtype).at[offset:offset + w.shape[0]].set(w)


def _row(v):
    return v.reshape(1, -1)


@jax.jit
def kernel(x, meta, emb_ln_g, emb_ln_b, w_in, w_out, pool_w, pool_scale, rw_mu, rw_w0, rw_w_up, rw_a0, rw_a_up, rw_g_up, rw_k_k, rw_k_a, rw_r_k, rw_gn_g, rw_gn_b, rw_v0, rw_v_down, rw_v_up, lru_conv_w, lru_conv_b, lru_ga_w, lru_ga_b, lru_gx_w, lru_gx_b, lru_lambda, ml_if_b, ml_gn_g, ml_gn_b, ln1_g, ln1_b, ln2_g, ln2_b, mlp_w1, mlp_w2):
    batch, seq, _ = x.shape
    depth = w_in.shape[0]
    assert seq % TB == 0 and x.shape[2] == D_MODEL
    tp = TB + seq
    alpha = (2 * depth) ** 0.25

    h = pl.pallas_call(
        _embed_ln_kernel,
        grid=(tp // TB,),
        in_specs=[pl.BlockSpec((batch, TB, D_MODEL), lambda t: (0, jnp.maximum(t - 1, 0), 0)),
                  pl.BlockSpec((N_META, D_MODEL), lambda t: (0, 0)),
                  pl.BlockSpec((1, D_MODEL), lambda t: (0, 0)),
                  pl.BlockSpec((1, D_MODEL), lambda t: (0, 0))],
        out_specs=pl.BlockSpec((batch, TB, D_MODEL), lambda t: (0, t, 0)),
        out_shape=jax.ShapeDtypeStruct((batch, tp, D_MODEL), F32),
        compiler_params=pltpu.CompilerParams(dimension_semantics=("arbitrary",)),
        name="embed_ln",
    )(x, meta.astype(x.dtype), _row(emb_ln_g), _row(emb_ln_b))

    w_in_b = jnp.pad(w_in, ((0, 0), (0, 0), (0, D_IN_PAD - w_in.shape[2]))).astype(BF16)
    w_out_b = w_out.astype(BF16)
    w1_b = mlp_w1.astype(BF16)
    w2_b = mlp_w2.astype(BF16)

    v_first = None
    for l in range(depth):
        wax = jnp.concatenate([_block_diag(lru_ga_w[l]), _block_diag(lru_gx_w[l])], axis=1).astype(BF16)
        bax = jnp.concatenate([lru_ga_b[l], lru_gx_b[l]]).reshape(1, 2 * GROUP_W)
        pool_lru_part = (
            _pool_lru_kernel, _POOL_LRU_STAGES, ("pool", "lru"), [],
            [_block_diag(pool_w[l]).astype(BF16), _row(pool_scale[l]), lru_conv_w[l], _row(lru_conv_b[l]),
             wax, bax, _row(lru_lambda[l])],
            (0, 2), [],
            [(TB + _POOL_OFF, GROUP_W)] * 4 + [(TB + SUBLANE, GROUP_W), (TB, GROUP_W),
                                               (TB + _SCAN_PAD, GROUP_W), (TB + _SCAN_PAD, GROUP_W),
                                               (1, GROUP_W)])

        ifb = jnp.zeros((1, LANE), F32).at[0, 0:2 * N_H].set(ml_if_b[l])
        mlstm_part = (
            _mlstm_kernel, _MLSTM_STAGES, ("ml",), [], [ifb, _row(ml_gn_g[l]), _row(ml_gn_b[l])], (3,), [],
            [(GROUP_W, GROUP_W), (1, GROUP_W), (1, LANE)])

        wup = _pad_rows(rw_w_up[l], LANE, 0).astype(BF16)
        aup = _pad_rows(rw_a_up[l], LANE, 32).astype(BF16)
        gup = _pad_rows(rw_g_up[l], LANE, 64).astype(BF16)
        rw_consts = [_row(rw_mu[l]), _row(rw_w0[l]), wup, _row(rw_a0[l]), aup, gup, _row(rw_k_k[l]),
                     _row(rw_k_a[l]), _row(rw_r_k[l]), _row(rw_gn_g[l]), _row(rw_gn_b[l])]
        rw_scratch = [(TB + SUBLANE, RW_COLS), (GROUP_W, GROUP_W)]
        post_consts = [w_out_b[l], _row(ln1_g[l]), _row(ln1_b[l]), w1_b[l], w2_b[l], _row(ln2_g[l]),
                       _row(ln2_b[l])]
        if l == 0:
            rwkv_part = (functools.partial(_rwkv_kernel, False), _RWKV_STAGES, ("rw",), [], rw_consts,
                         (1,), [GROUP_W], rw_scratch)
            h, v_first = _layer_call("layer_first", batch, tp, alpha, h, w_in_b[l], post_consts,
                                     [rwkv_part, mlstm_part, pool_lru_part], l == depth - 1)
        else:
            vdn = jnp.pad(rw_v_down[l - 1], ((0, 0), (0, LANE - rw_v_down.shape[2]))).astype(BF16)
            vup = _pad_rows(rw_v_up[l - 1], LANE, 0).astype(BF16)
            rwkv_part = (functools.partial(_rwkv_kernel, True), _RWKV_STAGES, ("rw",), [v_first],
                         rw_consts + [_row(rw_v0[l - 1]), vdn, vup], (1,), [], rw_scratch)
            h, = _layer_call("layer", batch, tp, alpha, h, w_in_b[l], post_consts,
                             [rwkv_part, mlstm_part, pool_lru_part], l == depth - 1)

    return h
```

```python
import functools
import math

import jax
import jax.numpy as jnp
from jax import lax
from jax.experimental import pallas as pl
from jax.experimental.pallas import tpu as pltpu

D_MODEL = 1024
N_META = 16
GROUP_W = 256
HEAD = 64
N_H = 4
POOL_MAX_W = 16
CONV_W = 4
LRU_C = 8.0
LN_EPS = 1e-5
GN_EPS = 64e-5
NEG = -1e30
D_FF = 4096
RW_COLS = 896
ML_COLS_PAD = 1152
D_IN_PAD = 2816
LANE = 128
SUBLANE = 8

TB = 256
RW_CHUNK = 64
VMEM_LIMIT = 60 * 1024 * 1024

F32 = jnp.float32
BF16 = jnp.bfloat16

_NN = (((1,), (0,)), ((), ()))
_NT = (((1,), (1,)), ((), ()))
_TN = (((0,), (0,)), ((), ()))


def _dot(a, b, dims=_NN):
    return lax.dot_general(a, b, dims, preferred_element_type=F32)


def _bdot(a, b, dims=_NN):
    return _dot(a.astype(BF16), b.astype(BF16), dims)


def _split3(x):
    hi = x.astype(BF16)
    r1 = x - hi.astype(F32)
    mid = r1.astype(BF16)
    lo = (r1 - mid.astype(F32)).astype(BF16)
    return hi, mid, lo


def _dot_exact_lhs(e_bf16, x, dims=_NN):
    hi, mid, lo = _split3(x)
    return _dot(e_bf16, hi, dims) + _dot(e_bf16, mid, dims) + _dot(e_bf16, lo, dims)


def _softplus(x):
    return jnp.maximum(x, 0.0) + jnp.log(1.0 + jnp.exp(-jnp.abs(x)))


def _sigmoid(x):
    return 0.5 * jnp.tanh(0.5 * x) + 0.5


def _softplus_small(x):
    return jnp.maximum(x, 0.0) + jnp.log1p(jnp.exp(-jnp.abs(x)))


def _layer_norm(x, g, b):
    mu = jnp.mean(x, axis=-1, keepdims=True)
    xc = x - mu
    var = jnp.mean(xc * xc, axis=-1, keepdims=True)
    return xc * lax.rsqrt(var + LN_EPS) * g + b


def _head_masks(width=GROUP_W):
    lane = lax.broadcasted_iota(jnp.int32, (1, width), 1)
    return [(lane // HEAD) == h for h in range(N_H)]


def _head_ones():
    rr = lax.broadcasted_iota(jnp.int32, (GROUP_W, 1), 0) // HEAD
    cc = lax.broadcasted_iota(jnp.int32, (1, GROUP_W), 1) // HEAD
    same = rr == cc
    return same, jnp.where(same, 1.0, 0.0).astype(BF16)


def _head_sum(x, ones_b):
    return _dot(x.astype(BF16), ones_b)


def _head_norm(y, g, b, ones_b):
    mu = _head_sum(y, ones_b) * (1.0 / HEAD)
    yc = y - mu
    var = _head_sum(yc * yc, ones_b) * (1.0 / HEAD)
    return yc * lax.rsqrt(var + GN_EPS) * g + b


def _col(x, j):
    lane = lax.broadcasted_iota(jnp.int32, (1, x.shape[1]), 1)
    return jnp.sum(jnp.where(lane == j, x, 0.0), axis=1, keepdims=True)


def _valid_rows(t):
    row = t * TB + lax.broadcasted_iota(jnp.int32, (TB, 1), 0)
    return row, row >= (TB - N_META)


def _embed_ln_kernel(x_ref, meta_ref, g_ref, b_ref, o_ref):
    t = pl.program_id(0)
    batch = o_ref.shape[0]

    @pl.when(t == 0)
    def _():
        meta_n = _layer_norm(meta_ref[...], g_ref[...], b_ref[...])
        for b in range(batch):
            o_ref[b, 0:TB - N_META, :] = jnp.zeros((TB - N_META, D_MODEL), F32)
            o_ref[b, TB - N_META:TB, :] = meta_n

    @pl.when(t > 0)
    def _():
        for b in range(batch):
            o_ref[b] = _layer_norm(x_ref[b], g_ref[...], b_ref[...])


_SCAN_PAD = TB // 2
_POOL_OFF = SUBLANE + POOL_MAX_W
_INIT, _BODY = "init", "body"
_DONE = object()
_LEAD_STAGES = 2
_RWKV_STAGES, _MLSTM_STAGES, _POOL_LRU_STAGES, _INPROJ_STAGES, _POST_STAGES = 22, 16, 17, 11, 38


def _pool_lru_kernel(phase, pp_ref, plru_ref, poolw_ref, pscale_ref, convw_ref, convb_ref, wax_ref, bax_ref,
                     lam_ref, ypool_ref, ylru_ref, pext, pw2, pw4, pw8, cext, gbuf, s_a, s_u, hcar):
    t = pl.program_id(1)
    if phase == _INIT:
        @pl.when(t == 0)
        def _():
            pext[0:_POOL_OFF, :] = jnp.zeros((_POOL_OFF, GROUP_W), F32)
            for buf in (pw2, pw4, pw8):
                buf[0:SUBLANE, :] = jnp.zeros((SUBLANE, GROUP_W), F32)
            cext[0:SUBLANE, :] = jnp.zeros((SUBLANE, GROUP_W), F32)
            hcar[...] = jnp.zeros((1, GROUP_W), F32)
            s_a[0:_SCAN_PAD, :] = jnp.ones((_SCAN_PAD, GROUP_W), F32)
            s_u[0:_SCAN_PAD, :] = jnp.zeros((_SCAN_PAD, GROUP_W), F32)
        return

    row, valid = _valid_rows(t)

    u = jnp.where(valid, pp_ref[...], 0.0)
    lo_p, hi_p = SUBLANE, _POOL_OFF + TB
    pext[_POOL_OFF:hi_p, :] = u
    cext[SUBLANE:SUBLANE + TB, :] = jnp.where(valid, plru_ref[:, 0:GROUP_W], 0.0)
    gbuf[...] = plru_ref[:, GROUP_W:2 * GROUP_W]
    pw2[lo_p:hi_p, :] = pext[lo_p:hi_p, :] + pext[lo_p - 1:hi_p - 1, :]
    yield
    pw4[lo_p:hi_p, :] = pw2[lo_p:hi_p, :] + pw2[lo_p - 2:hi_p - 2, :]
    yield
    pw8[lo_p:hi_p, :] = pw4[lo_p:hi_p, :] + pw4[lo_p - 4:hi_p - 4, :]
    yield
    w16 = pw8[_POOL_OFF:hi_p, :] + pw8[_POOL_OFF - 8:hi_p - 8, :]
    grp = lax.broadcasted_iota(jnp.int32, (1, GROUP_W), 1) // HEAD
    acc = jnp.where(grp == 0, pw2[_POOL_OFF:hi_p, :],
                    jnp.where(grp == 1, pw4[_POOL_OFF:hi_p, :],
                              jnp.where(grp == 2, pw8[_POOL_OFF:hi_p, :], w16)))
    win = jnp.left_shift(2, grp)
    pos1 = (row - (TB - N_META) + 1).astype(F32)
    cnt = jnp.clip(pos1, 1.0, win.astype(F32))
    dlt = acc / cnt - u
    ypool_ref[...] = (_bdot(dlt, poolw_ref[...]) * pscale_ref[...]).astype(ypool_ref.dtype)
    pext[lo_p:_POOL_OFF, :] = pext[TB + lo_p:TB + _POOL_OFF, :]
    yield

    xc = jnp.zeros((TB, GROUP_W), F32) + convb_ref[...]
    for j in range(CONV_W):
        off = SUBLANE - (CONV_W - 1) + j
        xc = xc + convw_ref[j:j + 1, :] * cext[off:off + TB, :]
    cext[0:SUBLANE, :] = cext[TB:TB + SUBLANE, :]
    yield
    z = _bdot(xc, wax_ref[...]) + bax_ref[...]
    r = _sigmoid(z[:, 0:GROUP_W])
    i = _sigmoid(z[:, GROUP_W:2 * GROUP_W])
    yield
    log_a = (-LRU_C) * r * _softplus_small(-lam_ref[...])
    a = jnp.exp(log_a)
    uu = jnp.where(valid, jnp.sqrt(-jnp.tanh(log_a) * (a * a + 1.0)) * (i * xc), 0.0)
    lo, hi = _SCAN_PAD, _SCAN_PAD + TB
    s_a[lo:hi, :] = a
    s_u[lo:hi, :] = uu
    yield
    d = 1
    while d < TB:
        a0, u0 = s_a[lo:hi, :], s_u[lo:hi, :]
        a1, u1 = s_a[lo - d:hi - d, :], s_u[lo - d:hi - d, :]
        s_u[lo:hi, :] = a0 * u1 + u0
        s_a[lo:hi, :] = a0 * a1
        d *= 2
        yield
    h = s_u[lo:hi, :] + s_a[lo:hi, :] * hcar[...]
    hcar[...] = h[TB - 1:TB, :]
    yield
    gate = gbuf[...]
    gcube = gate * gate * gate
    gelu = 0.5 * gate * (1.0 + jnp.tanh(math.sqrt(2.0 / math.pi) * (gate + 0.044715 * gcube)))
    ylru_ref[...] = (h * gelu).astype(ylru_ref.dtype)


def _mlstm_kernel(phase, p_ref, ifb_ref, gng_ref, gnb_ref, y_ref, c_st, n_st, m_st):
    t = pl.program_id(1)
    if phase == _INIT:
        @pl.when(t == 0)
        def _():
            c_st[...] = jnp.zeros((GROUP_W, GROUP_W), F32)
            n_st[...] = jnp.zeros((1, GROUP_W), F32)
            m_st[...] = jnp.zeros((1, LANE), F32)
        return

    _, valid = _valid_rows(t)
    hms = _head_masks()
    q = jnp.where(valid, p_ref[:, 0:256], 0.0) * (HEAD ** -0.5)
    k = jnp.where(valid, p_ref[:, 256:512], 0.0)
    v = jnp.where(valid, p_ref[:, 512:768], 0.0)
    o = p_ref[:, 768:1024]
    g = p_ref[:, 1024:ML_COLS_PAD] + ifb_ref[...]
    lane = lax.broadcasted_iota(jnp.int32, (1, LANE), 1)
    is_i = lane < N_H
    is_f = jnp.logical_and(lane >= N_H, lane < 2 * N_H)
    logf = jnp.where(jnp.logical_and(is_f, valid), -_softplus(-g), 0.0)
    logi = jnp.where(is_i, jnp.where(valid, g, NEG), 0.0)

    tcol = lax.broadcasted_iota(jnp.int32, (TB, 1), 0)
    srow = lax.broadcasted_iota(jnp.int32, (1, TB), 1)
    causal = srow <= tcol
    tri = jnp.where(causal, 1.0, 0.0).astype(BF16)
    bcum = _dot_exact_lhs(tri, logf)
    srow8 = lax.broadcasted_iota(jnp.int32, (SUBLANE, LANE), 0)
    lane8 = lax.broadcasted_iota(jnp.int32, (SUBLANE, LANE), 1)
    psel = (jnp.where(lane8 == srow8, 1.0, 0.0) - jnp.where(lane8 == srow8 + N_H, 1.0, 0.0)).astype(BF16)
    yield
    zt = _dot_exact_lhs(psel, logi + bcum, _NT)
    yield

    m_prev = m_st[...]
    m_inter_t = bcum + m_prev
    kb, vb = k.astype(BF16), v.astype(BF16)
    qc = _bdot(q, c_st[...])
    qn = q * n_st[...]

    num = jnp.zeros((TB, GROUP_W), F32)
    wk_e = jnp.zeros((TB, GROUP_W), F32)
    sc_row = jnp.zeros((1, GROUP_W), F32)
    m_new_t = jnp.zeros((1, LANE), F32)
    for h in range(N_H):
        hm = hms[h]
        b_h = _col(bcum, N_H + h)
        dmat = jnp.where(causal, b_h + zt[h:h + 1, :], NEG)
        m_intra = jnp.max(dmat, axis=1, keepdims=True)
        m_inter = _col(m_inter_t, N_H + h)
        m_t = jnp.maximum(m_intra, m_inter)
        sc = _dot(jnp.where(hm, q, 0.0).astype(BF16), kb, _NT)
        pm = sc * jnp.exp(dmat - m_t)
        yield
        num_h = _dot(pm.astype(BF16), jnp.where(hm, v, 0.0).astype(BF16))
        w_int = jnp.exp(m_inter - m_t)
        den = (jnp.sum(pm, axis=1, keepdims=True)
               + w_int * jnp.sum(jnp.where(hm, qn, 0.0), axis=1, keepdims=True))
        denom = jnp.maximum(jnp.abs(den), jnp.exp(-m_t))
        num = num + (num_h + w_int * jnp.where(hm, qc, 0.0)) * (1.0 / denom)
        yield
        b_last = b_h[TB - 1:TB, :]
        g_loc = b_last + (_col(logi, h) - b_h)
        m_loc = jnp.max(g_loc, axis=0, keepdims=True)
        m_old = jnp.sum(jnp.where(lane == N_H + h, m_prev, 0.0), axis=1, keepdims=True)
        m_new = jnp.maximum(b_last + m_old, m_loc)
        wk_e = wk_e + jnp.where(hm, jnp.exp(g_loc - m_new), 0.0)
        sc_row = sc_row + jnp.where(hm, jnp.exp(b_last + m_old - m_new), 0.0)
        m_new_t = m_new_t + jnp.where(lane == N_H + h, m_new, 0.0)
        yield

    kw = k * wk_e
    blockdiag, ones_b = _head_ones()
    c_st[...] = sc_row * c_st[...] + jnp.where(blockdiag, _dot(kw.astype(BF16), vb, _TN), 0.0)
    n_st[...] = sc_row * n_st[...] + jnp.sum(kw, axis=0, keepdims=True)
    m_st[...] = m_new_t
    yield

    y_ref[...] = (_head_norm(num, gng_ref[...], gnb_ref[...], ones_b) * _sigmoid(o)).astype(y_ref.dtype)


def _rwkv_kernel(has_vfirst, phase, *refs):
    if has_vfirst:
        (p_ref, vf_ref, mu_ref, w0_ref, wup_ref, a0_ref, aup_ref, gup_ref, kk_ref, ka_ref, rk_ref, gng_ref,
         gnb_ref, v0_ref, vdn_ref, vup_ref, y_ref, pext, s_st) = refs
    else:
        (p_ref, mu_ref, w0_ref, wup_ref, a0_ref, aup_ref, gup_ref, kk_ref, ka_ref, rk_ref, gng_ref,
         gnb_ref, y_ref, vfo_ref, pext, s_st) = refs
    t = pl.program_id(1)
    if phase == _INIT:
        @pl.when(t == 0)
        def _():
            pext[0:SUBLANE, :] = jnp.zeros((SUBLANE, RW_COLS), F32)
            s_st[...] = jnp.zeros((GROUP_W, GROUP_W), F32)
        return

    _, valid = _valid_rows(t)
    blockdiag, bd_b = _head_ones()
    p = jnp.where(valid, p_ref[...], 0.0)
    pext[SUBLANE:SUBLANE + TB, :] = p
    prev = pext[SUBLANE - 1:SUBLANE - 1 + TB, :]
    pext[0:SUBLANE, :] = pext[TB:TB + SUBLANE, :]
    pm = p + (prev - p) * mu_ref[...]
    yield
    r = pm[:, 0:256]
    k = pm[:, 256:512]
    v = pm[:, 512:768]
    lora = pm[:, 768:RW_COLS]
    ld = (-math.exp(-0.5)) * _sigmoid(w0_ref[...] + _bdot(jnp.tanh(lora), wup_ref[...]))
    a = _sigmoid(a0_ref[...] + _bdot(lora, aup_ref[...]))
    g = _bdot(_sigmoid(lora), gup_ref[...])
    yield
    if has_vfirst:
        vmix = _sigmoid(v0_ref[...] + _bdot(_bdot(v, vdn_ref[...]), vup_ref[...]))
        v = v + (vf_ref[...] - v) * vmix
    else:
        vfo_ref[...] = v
    kk = k * kk_ref[...]
    kk = kk * lax.rsqrt(_head_sum(kk * kk, bd_b) + 1e-12)
    k = k * (1.0 + (a - 1.0) * ka_ref[...])
    bvec = kk * a
    yield

    tcol = lax.broadcasted_iota(jnp.int32, (TB, 1), 0)
    srow = lax.broadcasted_iota(jnp.int32, (1, TB), 1)
    cum_mask = jnp.logical_and((tcol // RW_CHUNK) == (srow // RW_CHUNK), srow <= tcol)
    lc = _dot_exact_lhs(jnp.where(cum_mask, 1.0, 0.0).astype(BF16), ld)
    n_chunks = TB // RW_CHUNK
    chunks = range(n_chunks)
    lc_last = [lc[(c + 1) * RW_CHUNK - 1:(c + 1) * RW_CHUNK, :] for c in chunks]
    lc_end = jnp.concatenate([jnp.broadcast_to(x, (RW_CHUNK, GROUP_W)) for x in lc_last], axis=0)
    yield
    e_neg = jnp.exp(-lc)
    kap = (kk * jnp.exp(lc - ld)).astype(BF16)
    kt = (k * e_neg).astype(BF16)
    bt = (bvec * e_neg).astype(BF16)
    rt = r * jnp.exp(lc)
    rtb = rt.astype(BF16)
    vb = v.astype(BF16)
    e_end = jnp.exp(lc_end - lc)
    kbar = (k * e_end).astype(BF16)
    bbar = (bvec * e_end).astype(BF16)
    yield

    def bd(xb):
        return jnp.concatenate([xb] * N_H, axis=0) * bd_b

    tloc = lax.broadcasted_iota(jnp.int32, (RW_CHUNK, 1), 0)
    sloc = lax.broadcasted_iota(jnp.int32, (1, GROUP_W), 1) % RW_CHUNK
    strict = sloc < tloc
    incl = sloc <= tloc
    eye_pk = jnp.where(sloc == tloc, 1.0, 0.0)
    sls = [slice(c * RW_CHUNK, (c + 1) * RW_CHUNK) for c in chunks]

    lhs_kr = [jnp.concatenate([kap[sl], rtb[sl]], axis=0) for sl in sls]
    s_k = [_dot(lhs_kr[c], bd(kt[sls[c]]), _NT) for c in chunks]
    s_b = [_dot(lhs_kr[c], bd(bt[sls[c]]), _NT) for c in chunks]
    yield
    nmat = [jnp.where(strict, -s_b[c][0:RW_CHUNK], 0.0) for c in chunks]
    zs = [eye_pk + nmat[c] for c in chunks]
    ypow = [nmat[c].astype(BF16) for c in chunks]
    ypow = [_dot(ypow[c], bd(ypow[c])).astype(BF16) for c in chunks]
    n_iter = int(math.log2(RW_CHUNK)) - 1
    yield
    for it in range(n_iter):
        rhs = [bd(ypow[c]) for c in chunks]
        if it + 1 < n_iter:
            res = [_dot(jnp.concatenate([ypow[c], zs[c].astype(BF16)], axis=0), rhs[c]) for c in chunks]
            ypow = [res[c][0:RW_CHUNK].astype(BF16) for c in chunks]
            zs = [zs[c] + res[c][RW_CHUNK:2 * RW_CHUNK] for c in chunks]
        else:
            zs = [zs[c] + _dot(zs[c].astype(BF16), rhs[c]) for c in chunks]
        yield
    tinv = [zs[c].astype(BF16) for c in chunks]
    lhs_v = [jnp.concatenate([jnp.where(strict, s_k[c][0:RW_CHUNK], 0.0),
                              jnp.where(incl, s_k[c][RW_CHUNK:2 * RW_CHUNK], 0.0)], axis=0).astype(BF16)
             for c in chunks]
    res_v = [_dot(lhs_v[c], bd(vb[sls[c]])) for c in chunks]
    yield
    p1 = [_dot(tinv[c], bd(kap[sls[c]])) for c in chunks]
    p2 = [_dot(tinv[c], bd(res_v[c][0:RW_CHUNK].astype(BF16))) for c in chunks]
    yield
    a_rb = [jnp.where(incl, s_b[c][RW_CHUNK:2 * RW_CHUNK], 0.0).astype(BF16) for c in chunks]
    y0 = [res_v[c][RW_CHUNK:2 * RW_CHUNK] - _dot(a_rb[c], bd(p2[c].astype(BF16))) for c in chunks]
    qq = [rt[sls[c]] - _dot(a_rb[c], bd(p1[c].astype(BF16))) for c in chunks]
    yield

    s_val = s_st[...]
    ys = []
    for c in chunks:
        sl = sls[c]
        lhs = jnp.concatenate([p1[c], qq[c]], axis=0)
        res = _bdot(lhs, s_val, _NT)
        u_c = res[0:RW_CHUNK] + p2[c]
        ys.append(y0[c] + res[RW_CHUNK:2 * RW_CHUNK])
        lt = jnp.concatenate([vb[sl], (-u_c).astype(BF16)], axis=0)
        rt2 = jnp.concatenate([kbar[sl], bbar[sl]], axis=0)
        s_val = s_val * jnp.exp(lc_last[c]) + jnp.where(blockdiag, _dot(lt, rt2, _TN), 0.0)
        yield
    s_st[...] = s_val
    y = jnp.concatenate(ys, axis=0)

    yn = _head_norm(y, gng_ref[...], gnb_ref[...], bd_b)
    yield
    bonus = _head_sum(r * k * rk_ref[...], bd_b) * v
    y_ref[...] = ((yn + bonus) * g).astype(y_ref.dtype)


_P_SEGMENTS = (("pool", 0, 256), ("rw", 256, 1152), ("lru", 1152, 1664), ("ml", 1664, D_IN_PAD))


def _inproj_stages(h_ref, w_ref, p_refs):
    hb = h_ref[...].astype(BF16)
    for name, lo, hi in _P_SEGMENTS:
        for c0 in range(lo, hi, GROUP_W):
            c1 = min(c0 + GROUP_W, hi)
            p_refs[name][:, c0 - lo:c1 - lo] = _dot(hb, w_ref[:, c0:c1])
            yield


def _post_stages(alpha, y_s, h_ref, wo_ref, g1_ref, b1_ref, w1_ref, w2_ref, g2_ref, b2_ref, o_ref,
                 h1_s, hb_s, zb_s, acc_s):
    slabs = [slice(n * GROUP_W, (n + 1) * GROUP_W) for n in range(D_MODEL // GROUP_W)]
    for ns in slabs:
        h1_s[:, ns] = alpha * h_ref[:, ns] + _dot(y_s[...], wo_ref[:, ns])
        yield
    h1 = _layer_norm(h1_s[...], g1_ref[...], b1_ref[...])
    h1_s[...] = h1
    hb_s[...] = h1.astype(BF16)
    yield
    for j in range(D_FF // D_MODEL):
        for ns in slabs:
            z = jnp.maximum(_dot(hb_s[...], w1_ref[:, j * D_MODEL + ns.start:j * D_MODEL + ns.stop]), 0.0)
            zb_s[:, ns] = (z * z).astype(BF16)
            yield
        for ns in slabs:
            upd = _dot(zb_s[...], w2_ref[j * D_MODEL:(j + 1) * D_MODEL, ns])
            acc_s[:, ns] = upd if j == 0 else acc_s[:, ns] + upd
            yield
    o_ref[...] = _layer_norm(alpha * h1_s[...] + acc_s[...], g2_ref[...], b2_ref[...])


def _fused_layer_kernel(parts, alpha, *refs):
    h_next_ref, h_first_ref, h_prev_ref, w_ref = refs[:4]
    post_consts = refs[4:11]
    refs = refs[11:]
    n_in = sum(p[3] + p[4] for p in parts)
    n_out = 1 + sum(p[6] for p in parts)
    ins, outs, scr = refs[:n_in], refs[n_in:n_in + n_out], refs[n_in + n_out:]
    p_refs = {name: r for (name, _, _), r in zip(_P_SEGMENTS, scr)}
    y_s = scr[len(_P_SEGMENTS)]
    post_scr = scr[len(_P_SEGMENTS) + 1:len(_P_SEGMENTS) + 5]
    scr = scr[len(_P_SEGMENTS) + 5:]
    h_out_ref, outs = outs[0], outs[1:]

    @pl.when(pl.program_id(1) == 0)
    def _():
        y_s[...] = jnp.zeros(y_s.shape, y_s.dtype)
        for _ in _inproj_stages(h_first_ref.at[0], w_ref, p_refs):
            pass

    for phase in (_INIT, _BODY):
        gens, totals = [], []
        i = o = s = 0
        for fn, n_stages, pnames, nt, nc, y_slots, no, ns in parts:
            args = [p_refs[n] for n in pnames] + [r.at[0] for r in ins[i:i + nt]]
            args += list(ins[i + nt:i + nt + nc])
            args += [y_s.at[:, k * GROUP_W:(k + 1) * GROUP_W] for k in y_slots]
            args += [r.at[0] for r in outs[o:o + no]] + list(scr[s:s + ns])
            gens.append(fn(phase, *args))
            totals.append(n_stages)
            i, o, s = i + nt + nc, o + no, s + ns
        if phase == _BODY:
            gens.append(_inproj_stages(h_next_ref.at[0], w_ref, p_refs))
            totals.append(_INPROJ_STAGES)
            gens.append(_post_stages(alpha, y_s, h_prev_ref.at[0], *post_consts, h_out_ref.at[0], *post_scr))
            totals.append(_POST_STAGES)
        done = [0] * len(gens)

        def advance(j):
            if next(gens[j], _DONE) is _DONE:
                done[j] = None
            else:
                done[j] += 1

        for _ in range(_LEAD_STAGES):
            advance(0)
        for j in range(len(gens)):
            if done[j] == 0:
                advance(j)
        while any(d is not None for d in done):
            j = min((d / totals[k], k) for k, d in enumerate(done) if d is not None)[1]
            advance(j)


def _layer_call(name, batch, tp, alpha, h, w_in_b, post_consts, parts, drop_front):
    n_t = tp // TB
    skip = 1 if drop_front else 0

    def tspec(width, index):
        return pl.BlockSpec((1, TB, width), index)

    def cspec(shape):
        return pl.BlockSpec(shape, lambda b, t: (0,) * len(shape), pipeline_mode=pl.Buffered(1))

    in_specs = [tspec(D_MODEL, lambda b, t: (b, jnp.minimum(t + 1, n_t - 1), 0)),
                tspec(D_MODEL, lambda b, t: (b, 0, 0)),
                tspec(D_MODEL, lambda b, t: (b, jnp.maximum(t - 1, 0), 0)),
                cspec(w_in_b.shape)] + [cspec(a.shape) for a in post_consts]
    operands = [h, h, h, w_in_b] + list(post_consts)
    out_shape = [jax.ShapeDtypeStruct((batch, tp - skip * TB, D_MODEL), F32)]
    out_specs = [tspec(D_MODEL, lambda b, t: (b, jnp.maximum(t - 1 - skip, 0), 0))]
    scratch = [pltpu.VMEM((TB, hi - lo), F32) for _, lo, hi in _P_SEGMENTS]
    scratch += [pltpu.VMEM((TB, D_MODEL), BF16), pltpu.VMEM((TB, D_MODEL), F32), pltpu.VMEM((TB, D_MODEL), BF16),
                pltpu.VMEM((TB, D_MODEL), BF16), pltpu.VMEM((TB, D_MODEL), F32)]
    sig = []
    for fn, n_stages, pnames, time_ins, const_ins, y_slots, extra_widths, scr in parts:
        in_specs += [tspec(a.shape[2], lambda b, t: (b, jnp.minimum(t, n_t - 1), 0)) for a in time_ins]
        in_specs += [cspec(a.shape) for a in const_ins]
        operands += list(time_ins) + list(const_ins)
        out_shape += [jax.ShapeDtypeStruct((batch, tp + TB, w), F32) for w in extra_widths]
        out_specs += [tspec(w, lambda b, t: (b, t, 0)) for w in extra_widths]
        scratch += [pltpu.VMEM(tuple(shape), F32) for shape in scr]
        sig.append((fn, n_stages, tuple(pnames), len(time_ins), len(const_ins), tuple(y_slots),
                    len(extra_widths), len(scr)))
    return pl.pallas_call(
        functools.partial(_fused_layer_kernel, tuple(sig), alpha),
        grid=(batch, n_t + 1),
        in_specs=in_specs,
        out_specs=out_specs,
        out_shape=out_shape,
        scratch_shapes=scratch,
        compiler_params=pltpu.CompilerParams(dimension_semantics=("arbitrary", "arbitrary"),
                                             vmem_limit_bytes=VMEM_LIMIT),
        name=name,
    )(*operands)


def _block_diag(w):
    eye = jnp.eye(N_H, dtype=w.dtype)
    return jnp.einsum('gcd,gh->gchd', w, eye).reshape(GROUP_W, GROUP_W)


def _pad_rows(w, total, offset):
    return jnp.zeros((total, w.shape[1]), w.dtype).at[offset:offset + w.shape[0]].set(w)


def _row(v):
    return v.reshape(1, -1)


@jax.jit
def kernel(x, meta, emb_ln_g, emb_ln_b, w_in, w_out, pool_w, pool_scale, rw_mu, rw_w0, rw_w_up, rw_a0, rw_a_up, rw_g_up, rw_k_k, rw_k_a, rw_r_k, rw_gn_g, rw_gn_b, rw_v0, rw_v_down, rw_v_up, lru_conv_w, lru_conv_b, lru_ga_w, lru_ga_b, lru_gx_w, lru_gx_b, lru_lambda, ml_if_b, ml_gn_g, ml_gn_b, ln1_g, ln1_b, ln2_g, ln2_b, mlp_w1, mlp_w2):
    batch, seq, _ = x.shape
    depth = w_in.shape[0]
    assert seq % TB == 0 and x.shape[2] == D_MODEL
    tp = TB + seq
    alpha = (2 * depth) ** 0.25

    h = pl.pallas_call(
        _embed_ln_kernel,
        grid=(tp // TB,),
        in_specs=[pl.BlockSpec((batch, TB, D_MODEL), lambda t: (0, jnp.maximum(t - 1, 0), 0)),
                  pl.BlockSpec((N_META, D_MODEL), lambda t: (0, 0)),
                  pl.BlockSpec((1, D_MODEL), lambda t: (0, 0)),
                  pl.BlockSpec((1, D_MODEL), lambda t: (0, 0))],
        out_specs=pl.BlockSpec((batch, TB, D_MODEL), lambda t: (0, t, 0)),
        out_shape=jax.ShapeDtypeStruct((batch, tp, D_MODEL), F32),
        compiler_params=pltpu.CompilerParams(dimension_semantics=("arbitrary",)),
        name="embed_ln",
    )(x, meta.astype(x.dtype), _row(emb_ln_g), _row(emb_ln_b))

    v_first = None
    for l in range(depth):
        w_in_l = jnp.pad(w_in[l], ((0, 0), (0, D_IN_PAD - w_in.shape[2]))).astype(BF16)
        wax = jnp.concatenate([_block_diag(lru_ga_w[l]), _block_diag(lru_gx_w[l])], axis=1).astype(BF16)
        bax = jnp.concatenate([lru_ga_b[l], lru_gx_b[l]]).reshape(1, 2 * GROUP_W)
        pool_lru_part = (
            _pool_lru_kernel, _POOL_LRU_STAGES, ("pool", "lru"), [],
            [_block_diag(pool_w[l]).astype(BF16), _row(pool_scale[l]), lru_conv_w[l], _row(lru_conv_b[l]),
             wax, bax, _row(lru_lambda[l])],
            (0, 2), [],
            [(TB + _POOL_OFF, GROUP_W)] * 4 + [(TB + SUBLANE, GROUP_W), (TB, GROUP_W),
                                               (TB + _SCAN_PAD, GROUP_W), (TB + _SCAN_PAD, GROUP_W),
                                               (1, GROUP_W)])

        ifb = jnp.zeros((1, LANE), F32).at[0, 0:2 * N_H].set(ml_if_b[l])
        mlstm_part = (
            _mlstm_kernel, _MLSTM_STAGES, ("ml",), [], [ifb, _row(ml_gn_g[l]), _row(ml_gn_b[l])], (3,), [],
            [(GROUP_W, GROUP_W), (1, GROUP_W), (1, LANE)])

        wup = _pad_rows(rw_w_up[l], LANE, 0).astype(BF16)
        aup = _pad_rows(rw_a_up[l], LANE, 32).astype(BF16)
        gup = _pad_rows(rw_g_up[l], LANE, 64).astype(BF16)
        rw_consts = [_row(rw_mu[l]), _row(rw_w0[l]), wup, _row(rw_a0[l]), aup, gup, _row(rw_k_k[l]),
                     _row(rw_k_a[l]), _row(rw_r_k[l]), _row(rw_gn_g[l]), _row(rw_gn_b[l])]
        rw_scratch = [(TB + SUBLANE, RW_COLS), (GROUP_W, GROUP_W)]
        post_consts = [w_out[l].astype(BF16), _row(ln1_g[l]), _row(ln1_b[l]), mlp_w1[l].astype(BF16),
                       mlp_w2[l].astype(BF16), _row(ln2_g[l]), _row(ln2_b[l])]
        if l == 0:
            rwkv_part = (functools.partial(_rwkv_kernel, False), _RWKV_STAGES, ("rw",), [], rw_consts,
                         (1,), [GROUP_W], rw_scratch)
            h, v_first = _layer_call("layer_first", batch, tp, alpha, h, w_in_l, post_consts,
                                     [rwkv_part, mlstm_part, pool_lru_part], l == depth - 1)
        else:
            vdn = jnp.pad(rw_v_down[l - 1], ((0, 0), (0, LANE - rw_v_down.shape[2]))).astype(BF16)
            vup = _pad_rows(rw_v_up[l - 1], LANE, 0).astype(BF16)
            rwkv_part = (functools.partial(_rwkv_kernel, True), _RWKV_STAGES, ("rw",), [v_first],
                         rw_consts + [_row(rw_v0[l - 1]), vdn, vup], (1,), [], rw_scratch)
            h, = _layer_call("layer", batch, tp, alpha, h, w_in_l, post_consts,
                             [rwkv_part, mlstm_part, pool_lru_part], l == depth - 1)

    return h
```

```python
import functools
import math

import jax
import jax.numpy as jnp
from jax import lax
from jax.experimental import pallas as pl
from jax.experimental.pallas import tpu as pltpu

D_MODEL = 1024
N_META = 16
GROUP_W = 256
HEAD = 64
N_H = 4
POOL_MAX_W = 16
CONV_W = 4
LRU_C = 8.0
LN_EPS = 1e-5
GN_EPS = 64e-5
NEG = -1e30
D_FF = 4096
RW_COLS = 896
ML_COLS_PAD = 1152
D_IN_PAD = 2816
LANE = 128
SUBLANE = 8

TB = 256
RW_CHUNK = 64
VMEM_LIMIT = 60 * 1024 * 1024

F32 = jnp.float32
BF16 = jnp.bfloat16

_NN = (((1,), (0,)), ((), ()))
_NT = (((1,), (1,)), ((), ()))
_TN = (((0,), (0,)), ((), ()))


def _dot(a, b, dims=_NN):
    return lax.dot_general(a, b, dims, preferred_element_type=F32)


def _bdot(a, b, dims=_NN):
    return _dot(a.astype(BF16), b.astype(BF16), dims)


def _split3(x):
    hi = x.astype(BF16)
    r1 = x - hi.astype(F32)
    mid = r1.astype(BF16)
    lo = (r1 - mid.astype(F32)).astype(BF16)
    return hi, mid, lo


def _dot_exact_lhs(e_bf16, x, dims=_NN):
    hi, mid, lo = _split3(x)
    return _dot(e_bf16, hi, dims) + _dot(e_bf16, mid, dims) + _dot(e_bf16, lo, dims)


def _softplus(x):
    return jnp.maximum(x, 0.0) + jnp.log(1.0 + jnp.exp(-jnp.abs(x)))


def _sigmoid(x):
    return 0.5 * jnp.tanh(0.5 * x) + 0.5


def _softplus_small(x):
    return jnp.maximum(x, 0.0) + jnp.log1p(jnp.exp(-jnp.abs(x)))


def _layer_norm(x, g, b):
    mu = jnp.mean(x, axis=-1, keepdims=True)
    xc = x - mu
    var = jnp.mean(xc * xc, axis=-1, keepdims=True)
    return xc * lax.rsqrt(var + LN_EPS) * g + b


def _head_masks(width=GROUP_W):
    lane = lax.broadcasted_iota(jnp.int32, (1, width), 1)
    return [(lane // HEAD) == h for h in range(N_H)]


def _head_ones():
    rr = lax.broadcasted_iota(jnp.int32, (GROUP_W, 1), 0) // HEAD
    cc = lax.broadcasted_iota(jnp.int32, (1, GROUP_W), 1) // HEAD
    same = rr == cc
    return same, jnp.where(same, 1.0, 0.0).astype(BF16)


def _head_sum(x, ones_b):
    return _dot(x.astype(BF16), ones_b)


def _head_norm(y, g, b, ones_b):
    mu = _head_sum(y, ones_b) * (1.0 / HEAD)
    yc = y - mu
    var = _head_sum(yc * yc, ones_b) * (1.0 / HEAD)
    return yc * lax.rsqrt(var + GN_EPS) * g + b


def _col(x, j):
    lane = lax.broadcasted_iota(jnp.int32, (1, x.shape[1]), 1)
    return jnp.sum(jnp.where(lane == j, x, 0.0), axis=1, keepdims=True)


def _valid_rows(t):
    row = t * TB + lax.broadcasted_iota(jnp.int32, (TB, 1), 0)
    return row, row >= (TB - N_META)


def _embed_ln_kernel(x_ref, meta_ref, g_ref, b_ref, o_ref):
    t = pl.program_id(0)
    batch = o_ref.shape[0]

    @pl.when(t == 0)
    def _():
        meta_n = _layer_norm(meta_ref[...], g_ref[...], b_ref[...])
        for b in range(batch):
            o_ref[b, 0:TB - N_META, :] = jnp.zeros((TB - N_META, D_MODEL), F32)
            o_ref[b, TB - N_META:TB, :] = meta_n

    @pl.when(t > 0)
    def _():
        for b in range(batch):
            o_ref[b] = _layer_norm(x_ref[b], g_ref[...], b_ref[...])


_SCAN_PAD = TB // 2
_POOL_OFF = SUBLANE + POOL_MAX_W
_INIT, _BODY = "init", "body"
_DONE = object()
_LEAD_STAGES = 2
_RWKV_STAGES, _MLSTM_STAGES, _POOL_LRU_STAGES, _INPROJ_STAGES, _POST_STAGES = 22, 16, 17, 11, 38


def _pool_lru_kernel(n_t, phase, pp_ref, plru_ref, poolw_ref, pscale_ref, convw_ref, convb_ref, wax_ref, bax_ref,
                     lam_ref, ypool_ref, ylru_ref, pext, pw2, pw4, pw8, cext, gbuf, s_a, s_u, hcar):
    t = pl.program_id(0) % n_t
    if phase == _INIT:
        @pl.when(t == 0)
        def _():
            pext[0:_POOL_OFF, :] = jnp.zeros((_POOL_OFF, GROUP_W), F32)
            for buf in (pw2, pw4, pw8):
                buf[0:SUBLANE, :] = jnp.zeros((SUBLANE, GROUP_W), F32)
            cext[0:SUBLANE, :] = jnp.zeros((SUBLANE, GROUP_W), F32)
            hcar[...] = jnp.zeros((1, GROUP_W), F32)
            s_a[0:_SCAN_PAD, :] = jnp.ones((_SCAN_PAD, GROUP_W), F32)
            s_u[0:_SCAN_PAD, :] = jnp.zeros((_SCAN_PAD, GROUP_W), F32)
        return

    row, valid = _valid_rows(t)

    u = jnp.where(valid, pp_ref[...], 0.0)
    lo_p, hi_p = SUBLANE, _POOL_OFF + TB
    pext[_POOL_OFF:hi_p, :] = u
    cext[SUBLANE:SUBLANE + TB, :] = jnp.where(valid, plru_ref[:, 0:GROUP_W], 0.0)
    gbuf[...] = plru_ref[:, GROUP_W:2 * GROUP_W]
    pw2[lo_p:hi_p, :] = pext[lo_p:hi_p, :] + pext[lo_p - 1:hi_p - 1, :]
    yield
    pw4[lo_p:hi_p, :] = pw2[lo_p:hi_p, :] + pw2[lo_p - 2:hi_p - 2, :]
    yield
    pw8[lo_p:hi_p, :] = pw4[lo_p:hi_p, :] + pw4[lo_p - 4:hi_p - 4, :]
    yield
    w16 = pw8[_POOL_OFF:hi_p, :] + pw8[_POOL_OFF - 8:hi_p - 8, :]
    grp = lax.broadcasted_iota(jnp.int32, (1, GROUP_W), 1) // HEAD
    acc = jnp.where(grp == 0, pw2[_POOL_OFF:hi_p, :],
                    jnp.where(grp == 1, pw4[_POOL_OFF:hi_p, :],
                              jnp.where(grp == 2, pw8[_POOL_OFF:hi_p, :], w16)))
    win = jnp.left_shift(2, grp)
    pos1 = (row - (TB - N_META) + 1).astype(F32)
    cnt = jnp.clip(pos1, 1.0, win.astype(F32))
    dlt = acc / cnt - u
    ypool_ref[...] = (_bdot(dlt, poolw_ref[...]) * pscale_ref[...]).astype(ypool_ref.dtype)
    pext[lo_p:_POOL_OFF, :] = pext[TB + lo_p:TB + _POOL_OFF, :]
    yield

    xc = jnp.zeros((TB, GROUP_W), F32) + convb_ref[...]
    for j in range(CONV_W):
        off = SUBLANE - (CONV_W - 1) + j
        xc = xc + convw_ref[j:j + 1, :] * cext[off:off + TB, :]
    cext[0:SUBLANE, :] = cext[TB:TB + SUBLANE, :]
    yield
    z = _bdot(xc, wax_ref[...]) + bax_ref[...]
    r = _sigmoid(z[:, 0:GROUP_W])
    i = _sigmoid(z[:, GROUP_W:2 * GROUP_W])
    yield
    log_a = (-LRU_C) * r * _softplus_small(-lam_ref[...])
    a = jnp.exp(log_a)
    uu = jnp.where(valid, jnp.sqrt(-jnp.tanh(log_a) * (a * a + 1.0)) * (i * xc), 0.0)
    lo, hi = _SCAN_PAD, _SCAN_PAD + TB
    s_a[lo:hi, :] = a
    s_u[lo:hi, :] = uu
    yield
    d = 1
    while d < TB:
        a0, u0 = s_a[lo:hi, :], s_u[lo:hi, :]
        a1, u1 = s_a[lo - d:hi - d, :], s_u[lo - d:hi - d, :]
        s_u[lo:hi, :] = a0 * u1 + u0
        s_a[lo:hi, :] = a0 * a1
        d *= 2
        yield
    h = s_u[lo:hi, :] + s_a[lo:hi, :] * hcar[...]
    hcar[...] = h[TB - 1:TB, :]
    yield
    gate = gbuf[...]
    gcube = gate * gate * gate
    gelu = 0.5 * gate * (1.0 + jnp.tanh(math.sqrt(2.0 / math.pi) * (gate + 0.044715 * gcube)))
    ylru_ref[...] = (h * gelu).astype(ylru_ref.dtype)


def _mlstm_kernel(n_t, phase, p_ref, ifb_ref, gng_ref, gnb_ref, y_ref, c_st, n_st, m_st):
    t = pl.program_id(0) % n_t
    if phase == _INIT:
        @pl.when(t == 0)
        def _():
            c_st[...] = jnp.zeros((GROUP_W, GROUP_W), F32)
            n_st[...] = jnp.zeros((1, GROUP_W), F32)
            m_st[...] = jnp.zeros((1, LANE), F32)
        return

    _, valid = _valid_rows(t)
    hms = _head_masks()
    q = jnp.where(valid, p_ref[:, 0:256], 0.0) * (HEAD ** -0.5)
    k = jnp.where(valid, p_ref[:, 256:512], 0.0)
    v = jnp.where(valid, p_ref[:, 512:768], 0.0)
    o = p_ref[:, 768:1024]
    g = p_ref[:, 1024:ML_COLS_PAD] + ifb_ref[...]
    lane = lax.broadcasted_iota(jnp.int32, (1, LANE), 1)
    is_i = lane < N_H
    is_f = jnp.logical_and(lane >= N_H, lane < 2 * N_H)
    logf = jnp.where(jnp.logical_and(is_f, valid), -_softplus(-g), 0.0)
    logi = jnp.where(is_i, jnp.where(valid, g, NEG), 0.0)

    tcol = lax.broadcasted_iota(jnp.int32, (TB, 1), 0)
    srow = lax.broadcasted_iota(jnp.int32, (1, TB), 1)
    causal = srow <= tcol
    tri = jnp.where(causal, 1.0, 0.0).astype(BF16)
    bcum = _dot_exact_lhs(tri, logf)
    srow8 = lax.broadcasted_iota(jnp.int32, (SUBLANE, LANE), 0)
    lane8 = lax.broadcasted_iota(jnp.int32, (SUBLANE, LANE), 1)
    psel = (jnp.where(lane8 == srow8, 1.0, 0.0) - jnp.where(lane8 == srow8 + N_H, 1.0, 0.0)).astype(BF16)
    yield
    zt = _dot_exact_lhs(psel, logi + bcum, _NT)
    yield

    m_prev = m_st[...]
    m_inter_t = bcum + m_prev
    kb, vb = k.astype(BF16), v.astype(BF16)
    qc = _bdot(q, c_st[...])
    qn = q * n_st[...]

    num = jnp.zeros((TB, GROUP_W), F32)
    wk_e = jnp.zeros((TB, GROUP_W), F32)
    sc_row = jnp.zeros((1, GROUP_W), F32)
    m_new_t = jnp.zeros((1, LANE), F32)
    for h in range(N_H):
        hm = hms[h]
        b_h = _col(bcum, N_H + h)
        dmat = jnp.where(causal, b_h + zt[h:h + 1, :], NEG)
        m_intra = jnp.max(dmat, axis=1, keepdims=True)
        m_inter = _col(m_inter_t, N_H + h)
        m_t = jnp.maximum(m_intra, m_inter)
        sc = _dot(jnp.where(hm, q, 0.0).astype(BF16), kb, _NT)
        pm = sc * jnp.exp(dmat - m_t)
        yield
        num_h = _dot(pm.astype(BF16), jnp.where(hm, v, 0.0).astype(BF16))
        w_int = jnp.exp(m_inter - m_t)
        den = (jnp.sum(pm, axis=1, keepdims=True)
               + w_int * jnp.sum(jnp.where(hm, qn, 0.0), axis=1, keepdims=True))
        denom = jnp.maximum(jnp.abs(den), jnp.exp(-m_t))
        num = num + (num_h + w_int * jnp.where(hm, qc, 0.0)) * (1.0 / denom)
        yield
        b_last = b_h[TB - 1:TB, :]
        g_loc = b_last + (_col(logi, h) - b_h)
        m_loc = jnp.max(g_loc, axis=0, keepdims=True)
        m_old = jnp.sum(jnp.where(lane == N_H + h, m_prev, 0.0), axis=1, keepdims=True)
        m_new = jnp.maximum(b_last + m_old, m_loc)
        wk_e = wk_e + jnp.where(hm, jnp.exp(g_loc - m_new), 0.0)
        sc_row = sc_row + jnp.where(hm, jnp.exp(b_last + m_old - m_new), 0.0)
        m_new_t = m_new_t + jnp.where(lane == N_H + h, m_new, 0.0)
        yield

    kw = k * wk_e
    blockdiag, ones_b = _head_ones()
    c_st[...] = sc_row * c_st[...] + jnp.where(blockdiag, _dot(kw.astype(BF16), vb, _TN), 0.0)
    n_st[...] = sc_row * n_st[...] + jnp.sum(kw, axis=0, keepdims=True)
    m_st[...] = m_new_t
    yield

    y_ref[...] = (_head_norm(num, gng_ref[...], gnb_ref[...], ones_b) * _sigmoid(o)).astype(y_ref.dtype)


def _rwkv_kernel(has_vfirst, n_t, phase, *refs):
    if has_vfirst:
        (p_ref, vf_ref, mu_ref, w0_ref, wup_ref, a0_ref, aup_ref, gup_ref, kk_ref, ka_ref, rk_ref, gng_ref,
         gnb_ref, v0_ref, vdn_ref, vup_ref, y_ref, pext, s_st) = refs
    else:
        (p_ref, mu_ref, w0_ref, wup_ref, a0_ref, aup_ref, gup_ref, kk_ref, ka_ref, rk_ref, gng_ref,
         gnb_ref, y_ref, vfo_ref, pext, s_st) = refs
    t = pl.program_id(0) % n_t
    if phase == _INIT:
        @pl.when(t == 0)
        def _():
            pext[0:SUBLANE, :] = jnp.zeros((SUBLANE, RW_COLS), F32)
            s_st[...] = jnp.zeros((GROUP_W, GROUP_W), F32)
        return

    _, valid = _valid_rows(t)
    blockdiag, bd_b = _head_ones()
    p = jnp.where(valid, p_ref[...], 0.0)
    pext[SUBLANE:SUBLANE + TB, :] = p
    prev = pext[SUBLANE - 1:SUBLANE - 1 + TB, :]
    pext[0:SUBLANE, :] = pext[TB:TB + SUBLANE, :]
    pm = p + (prev - p) * mu_ref[...]
    yield
    r = pm[:, 0:256]
    k = pm[:, 256:512]
    v = pm[:, 512:768]
    lora = pm[:, 768:RW_COLS]
    ld = (-math.exp(-0.5)) * _sigmoid(w0_ref[...] + _bdot(jnp.tanh(lora), wup_ref[...]))
    a = _sigmoid(a0_ref[...] + _bdot(lora, aup_ref[...]))
    g = _bdot(_sigmoid(lora), gup_ref[...])
    yield
    if has_vfirst:
        vmix = _sigmoid(v0_ref[...] + _bdot(_bdot(v, vdn_ref[...]), vup_ref[...]))
        v = v + (vf_ref[...] - v) * vmix
    else:
        vfo_ref[...] = v
    kk = k * kk_ref[...]
    kk = kk * lax.rsqrt(_head_sum(kk * kk, bd_b) + 1e-12)
    k = k * (1.0 + (a - 1.0) * ka_ref[...])
    bvec = kk * a
    yield

    tcol = lax.broadcasted_iota(jnp.int32, (TB, 1), 0)
    srow = lax.broadcasted_iota(jnp.int32, (1, TB), 1)
    cum_mask = jnp.logical_and((tcol // RW_CHUNK) == (srow // RW_CHUNK), srow <= tcol)
    lc = _dot_exact_lhs(jnp.where(cum_mask, 1.0, 0.0).astype(BF16), ld)
    n_chunks = TB // RW_CHUNK
    chunks = range(n_chunks)
    lc_last = [lc[(c + 1) * RW_CHUNK - 1:(c + 1) * RW_CHUNK, :] for c in chunks]
    lc_end = jnp.concatenate([jnp.broadcast_to(x, (RW_CHUNK, GROUP_W)) for x in lc_last], axis=0)
    yield
    e_neg = jnp.exp(-lc)
    kap = (kk * jnp.exp(lc - ld)).astype(BF16)
    kt = (k * e_neg).astype(BF16)
    bt = (bvec * e_neg).astype(BF16)
    rt = r * jnp.exp(lc)
    rtb = rt.astype(BF16)
    vb = v.astype(BF16)
    e_end = jnp.exp(lc_end - lc)
    kbar = (k * e_end).astype(BF16)
    bbar = (bvec * e_end).astype(BF16)
    yield

    def bd(xb):
        return jnp.concatenate([xb] * N_H, axis=0) * bd_b

    tloc = lax.broadcasted_iota(jnp.int32, (RW_CHUNK, 1), 0)
    sloc = lax.broadcasted_iota(jnp.int32, (1, GROUP_W), 1) % RW_CHUNK
    strict = sloc < tloc
    incl = sloc <= tloc
    eye_pk = jnp.where(sloc == tloc, 1.0, 0.0)
    sls = [slice(c * RW_CHUNK, (c + 1) * RW_CHUNK) for c in chunks]

    lhs_kr = [jnp.concatenate([kap[sl], rtb[sl]], axis=0) for sl in sls]
    s_k = [_dot(lhs_kr[c], bd(kt[sls[c]]), _NT) for c in chunks]
    s_b = [_dot(lhs_kr[c], bd(bt[sls[c]]), _NT) for c in chunks]
    yield
    nmat = [jnp.where(strict, -s_b[c][0:RW_CHUNK], 0.0) for c in chunks]
    zs = [eye_pk + nmat[c] for c in chunks]
    ypow = [nmat[c].astype(BF16) for c in chunks]
    ypow = [_dot(ypow[c], bd(ypow[c])).astype(BF16) for c in chunks]
    n_iter = int(math.log2(RW_CHUNK)) - 1
    yield
    for it in range(n_iter):
        rhs = [bd(ypow[c]) for c in chunks]
        if it + 1 < n_iter:
            res = [_dot(jnp.concatenate([ypow[c], zs[c].astype(BF16)], axis=0), rhs[c]) for c in chunks]
            ypow = [res[c][0:RW_CHUNK].astype(BF16) for c in chunks]
            zs = [zs[c] + res[c][RW_CHUNK:2 * RW_CHUNK] for c in chunks]
        else:
            zs = [zs[c] + _dot(zs[c].astype(BF16), rhs[c]) for c in chunks]
        yield
    tinv = [zs[c].astype(BF16) for c in chunks]
    lhs_v = [jnp.concatenate([jnp.where(strict, s_k[c][0:RW_CHUNK], 0.0),
                              jnp.where(incl, s_k[c][RW_CHUNK:2 * RW_CHUNK], 0.0)], axis=0).astype(BF16)
             for c in chunks]
    res_v = [_dot(lhs_v[c], bd(vb[sls[c]])) for c in chunks]
    yield
    p1 = [_dot(tinv[c], bd(kap[sls[c]])) for c in chunks]
    p2 = [_dot(tinv[c], bd(res_v[c][0:RW_CHUNK].astype(BF16))) for c in chunks]
    yield
    a_rb = [jnp.where(incl, s_b[c][RW_CHUNK:2 * RW_CHUNK], 0.0).astype(BF16) for c in chunks]
    y0 = [res_v[c][RW_CHUNK:2 * RW_CHUNK] - _dot(a_rb[c], bd(p2[c].astype(BF16))) for c in chunks]
    qq = [rt[sls[c]] - _dot(a_rb[c], bd(p1[c].astype(BF16))) for c in chunks]
    yield

    s_val = s_st[...]
    ys = []
    for c in chunks:
        sl = sls[c]
        lhs = jnp.concatenate([p1[c], qq[c]], axis=0)
        res = _bdot(lhs, s_val, _NT)
        u_c = res[0:RW_CHUNK] + p2[c]
        ys.append(y0[c] + res[RW_CHUNK:2 * RW_CHUNK])
        lt = jnp.concatenate([vb[sl], (-u_c).astype(BF16)], axis=0)
        rt2 = jnp.concatenate([kbar[sl], bbar[sl]], axis=0)
        s_val = s_val * jnp.exp(lc_last[c]) + jnp.where(blockdiag, _dot(lt, rt2, _TN), 0.0)
        yield
    s_st[...] = s_val
    y = jnp.concatenate(ys, axis=0)

    yn = _head_norm(y, gng_ref[...], gnb_ref[...], bd_b)
    yield
    bonus = _head_sum(r * k * rk_ref[...], bd_b) * v
    y_ref[...] = ((yn + bonus) * g).astype(y_ref.dtype)


_P_SEGMENTS = (("pool", 0, 256), ("rw", 256, 1152), ("lru", 1152, 1664), ("ml", 1664, D_IN_PAD))


def _inproj_stages(h_ref, w_ref, p_refs):
    hb = h_ref[...].astype(BF16)
    for name, lo, hi in _P_SEGMENTS:
        for c0 in range(lo, hi, GROUP_W):
            c1 = min(c0 + GROUP_W, hi)
            p_refs[name][:, c0 - lo:c1 - lo] = _dot(hb, w_ref[:, c0:c1])
            yield


def _post_stages(alpha, y_s, h_ref, wo_ref, g1_ref, b1_ref, w1_ref, w2_ref, g2_ref, b2_ref, o_ref,
                 h1_s, hb_s, zb_s, acc_s):
    slabs = [slice(n * GROUP_W, (n + 1) * GROUP_W) for n in range(D_MODEL // GROUP_W)]
    for ns in slabs:
        h1_s[:, ns] = alpha * h_ref[:, ns] + _dot(y_s[...], wo_ref[:, ns])
        yield
    h1 = _layer_norm(h1_s[...], g1_ref[...], b1_ref[...])
    h1_s[...] = h1
    hb_s[...] = h1.astype(BF16)
    yield
    for j in range(D_FF // D_MODEL):
        for ns in slabs:
            z = jnp.maximum(_dot(hb_s[...], w1_ref[:, j * D_MODEL + ns.start:j * D_MODEL + ns.stop]), 0.0)
            zb_s[:, ns] = (z * z).astype(BF16)
            yield
        for ns in slabs:
            upd = _dot(zb_s[...], w2_ref[j * D_MODEL:(j + 1) * D_MODEL, ns])
            acc_s[:, ns] = upd if j == 0 else acc_s[:, ns] + upd
            yield
    o_ref[...] = _layer_norm(alpha * h1_s[...] + acc_s[...], g2_ref[...], b2_ref[...])


def _fused_layer_kernel(parts, alpha, *refs):
    h_next_ref, h_first_ref, h_prev_ref, w_ref = refs[:4]
    post_consts = refs[4:11]
    refs = refs[11:]
    n_in = sum(p[3] + p[4] for p in parts)
    n_out = 1 + sum(p[6] for p in parts)
    ins, outs, scr = refs[:n_in], refs[n_in:n_in + n_out], refs[n_in + n_out:]
    p_refs = {name: r for (name, _, _), r in zip(_P_SEGMENTS, scr)}
    y_s = scr[len(_P_SEGMENTS)]
    post_scr = scr[len(_P_SEGMENTS) + 1:len(_P_SEGMENTS) + 5]
    scr = scr[len(_P_SEGMENTS) + 5:]
    h_out_ref, outs = outs[0], outs[1:]

    @pl.when(pl.program_id(0) == 0)
    def _():
        y_s[...] = jnp.zeros(y_s.shape, y_s.dtype)
        for _ in _inproj_stages(h_first_ref.at[0], w_ref, p_refs):
            pass

    for phase in (_INIT, _BODY):
        gens, totals = [], []
        i = o = s = 0
        for fn, n_stages, pnames, nt, nc, y_slots, no, ns in parts:
            args = [p_refs[n] for n in pnames] + [r.at[0] for r in ins[i:i + nt]]
            args += list(ins[i + nt:i + nt + nc])
            args += [y_s.at[:, k * GROUP_W:(k + 1) * GROUP_W] for k in y_slots]
            args += [r.at[0] for r in outs[o:o + no]] + list(scr[s:s + ns])
            gens.append(fn(phase, *args))
            totals.append(n_stages)
            i, o, s = i + nt + nc, o + no, s + ns
        if phase == _BODY:
            gens.append(_inproj_stages(h_next_ref.at[0], w_ref, p_refs))
            totals.append(_INPROJ_STAGES)
            gens.append(_post_stages(alpha, y_s, h_prev_ref.at[0], *post_consts, h_out_ref.at[0], *post_scr))
            totals.append(_POST_STAGES)
        done = [0] * len(gens)

        def advance(j):
            if next(gens[j], _DONE) is _DONE:
                done[j] = None
            else:
                done[j] += 1

        for _ in range(_LEAD_STAGES):
            advance(0)
        for j in range(len(gens)):
            if done[j] == 0:
                advance(j)
        while any(d is not None for d in done):
            j = min((d / totals[k], k) for k, d in enumerate(done) if d is not None)[1]
            advance(j)


def _layer_call(name, batch, tp, alpha, h, w_in_b, post_consts, parts, drop_front):
    n_t = tp // TB
    n_blk = batch * n_t

    def tspec(width, index):
        return pl.BlockSpec((1, TB, width), index)

    def cspec(shape):
        return pl.BlockSpec(shape, lambda s: (0,) * len(shape), pipeline_mode=pl.Buffered(1))

    def out_index(s):
        blk = jnp.maximum(s - 1, 0)
        if not drop_front:
            return (blk, 0, 0)
        return ((blk // n_t) * (n_t - 1) + jnp.maximum(blk % n_t - 1, 0), 0, 0)

    in_specs = [tspec(D_MODEL, lambda s: (jnp.minimum(s + 1, n_blk - 1), 0, 0)),
                tspec(D_MODEL, lambda s: (0, 0, 0)),
                tspec(D_MODEL, lambda s: (jnp.maximum(s - 1, 0), 0, 0)),
                cspec(w_in_b.shape)] + [cspec(a.shape) for a in post_consts]
    operands = [h, h, h, w_in_b] + list(post_consts)
    out_shape = [jax.ShapeDtypeStruct((batch * (n_t - 1) if drop_front else n_blk, TB, D_MODEL), F32)]
    out_specs = [tspec(D_MODEL, out_index)]
    scratch = [pltpu.VMEM((TB, hi - lo), F32) for _, lo, hi in _P_SEGMENTS]
    scratch += [pltpu.VMEM((TB, D_MODEL), BF16), pltpu.VMEM((TB, D_MODEL), F32), pltpu.VMEM((TB, D_MODEL), BF16),
                pltpu.VMEM((TB, D_MODEL), BF16), pltpu.VMEM((TB, D_MODEL), F32)]
    sig = []
    for fn, n_stages, pnames, time_ins, const_ins, y_slots, extra_widths, scr in parts:
        in_specs += [tspec(a.shape[2], lambda s: (jnp.minimum(s, n_blk - 1), 0, 0)) for a in time_ins]
        in_specs += [cspec(a.shape) for a in const_ins]
        operands += list(time_ins) + list(const_ins)
        out_shape += [jax.ShapeDtypeStruct((n_blk + 1, TB, w), F32) for w in extra_widths]
        out_specs += [tspec(w, lambda s: (s, 0, 0)) for w in extra_widths]
        scratch += [pltpu.VMEM(tuple(shape), F32) for shape in scr]
        sig.append((functools.partial(fn, n_t), n_stages, tuple(pnames), len(time_ins), len(const_ins),
                    tuple(y_slots), len(extra_widths), len(scr)))
    return pl.pallas_call(
        functools.partial(_fused_layer_kernel, tuple(sig), alpha),
        grid=(n_blk + 1,),
        in_specs=in_specs,
        out_specs=out_specs,
        out_shape=out_shape,
        scratch_shapes=scratch,
        compiler_params=pltpu.CompilerParams(dimension_semantics=("arbitrary",),
                                             vmem_limit_bytes=VMEM_LIMIT),
        name=name,
    )(*operands)


def _block_diag(w):
    eye = jnp.eye(N_H, dtype=w.dtype)
    return jnp.einsum('gcd,gh->gchd', w, eye).reshape(GROUP_W, GROUP_W)


def _pad_rows(w, total, offset):
    return jnp.zeros((total, w.shape[1]), w.dtype).at[offset:offset + w.shape[0]].set(w)


def _row(v):
    return v.reshape(1, -1)


@jax.jit
def kernel(x, meta, emb_ln_g, emb_ln_b, w_in, w_out, pool_w, pool_scale, rw_mu, rw_w0, rw_w_up, rw_a0, rw_a_up, rw_g_up, rw_k_k, rw_k_a, rw_r_k, rw_gn_g, rw_gn_b, rw_v0, rw_v_down, rw_v_up, lru_conv_w, lru_conv_b, lru_ga_w, lru_ga_b, lru_gx_w, lru_gx_b, lru_lambda, ml_if_b, ml_gn_g, ml_gn_b, ln1_g, ln1_b, ln2_g, ln2_b, mlp_w1, mlp_w2):
    batch, seq, _ = x.shape
    depth = w_in.shape[0]
    assert seq % TB == 0 and x.shape[2] == D_MODEL
    tp = TB + seq
    alpha = (2 * depth) ** 0.25

    h = pl.pallas_call(
        _embed_ln_kernel,
        grid=(tp // TB,),
        in_specs=[pl.BlockSpec((batch, TB, D_MODEL), lambda t: (0, jnp.maximum(t - 1, 0), 0)),
                  pl.BlockSpec((N_META, D_MODEL), lambda t: (0, 0)),
                  pl.BlockSpec((1, D_MODEL), lambda t: (0, 0)),
                  pl.BlockSpec((1, D_MODEL), lambda t: (0, 0))],
        out_specs=pl.BlockSpec((batch, TB, D_MODEL), lambda t: (0, t, 0)),
        out_shape=jax.ShapeDtypeStruct((batch, tp, D_MODEL), F32),
        compiler_params=pltpu.CompilerParams(dimension_semantics=("arbitrary",)),
        name="embed_ln",
    )(x, meta.astype(x.dtype), _row(emb_ln_g), _row(emb_ln_b)).reshape(batch * (tp // TB), TB, D_MODEL)

    w_in_b = jnp.pad(w_in, ((0, 0), (0, 0), (0, D_IN_PAD - w_in.shape[2]))).astype(BF16)
    w_out_b = w_out.astype(BF16)
    w1_b = mlp_w1.astype(BF16)
    w2_b = mlp_w2.astype(BF16)

    v_first = None
    for l in range(depth):
        wax = jnp.concatenate([_block_diag(lru_ga_w[l]), _block_diag(lru_gx_w[l])], axis=1).astype(BF16)
        bax = jnp.concatenate([lru_ga_b[l], lru_gx_b[l]]).reshape(1, 2 * GROUP_W)
        pool_lru_part = (
            _pool_lru_kernel, _POOL_LRU_STAGES, ("pool", "lru"), [],
            [_block_diag(pool_w[l]).astype(BF16), _row(pool_scale[l]), lru_conv_w[l], _row(lru_conv_b[l]),
             wax, bax, _row(lru_lambda[l])],
            (0, 2), [],
            [(TB + _POOL_OFF, GROUP_W)] * 4 + [(TB + SUBLANE, GROUP_W), (TB, GROUP_W),
                                               (TB + _SCAN_PAD, GROUP_W), (TB + _SCAN_PAD, GROUP_W),
                                               (1, GROUP_W)])

        ifb = jnp.zeros((1, LANE), F32).at[0, 0:2 * N_H].set(ml_if_b[l])
        mlstm_part = (
            _mlstm_kernel, _MLSTM_STAGES, ("ml",), [], [ifb, _row(ml_gn_g[l]), _row(ml_gn_b[l])], (3,), [],
            [(GROUP_W, GROUP_W), (1, GROUP_W), (1, LANE)])

        wup = _pad_rows(rw_w_up[l], LANE, 0).astype(BF16)
        aup = _pad_rows(rw_a_up[l], LANE, 32).astype(BF16)
        gup = _pad_rows(rw_g_up[l], LANE, 64).astype(BF16)
        rw_consts = [_row(rw_mu[l]), _row(rw_w0[l]), wup, _row(rw_a0[l]), aup, gup, _row(rw_k_k[l]),
                     _row(rw_k_a[l]), _row(rw_r_k[l]), _row(rw_gn_g[l]), _row(rw_gn_b[l])]
        rw_scratch = [(TB + SUBLANE, RW_COLS), (GROUP_W, GROUP_W)]
        post_consts = [w_out_b[l], _row(ln1_g[l]), _row(ln1_b[l]), w1_b[l], w2_b[l], _row(ln2_g[l]),
                       _row(ln2_b[l])]
        if l == 0:
            rwkv_part = (functools.partial(_rwkv_kernel, False), _RWKV_STAGES, ("rw",), [], rw_consts,
                         (1,), [GROUP_W], rw_scratch)
            h, v_first = _layer_call("layer_first", batch, tp, alpha, h, w_in_b[l], post_consts,
                                     [rwkv_part, mlstm_part, pool_lru_part], l == depth - 1)
        else:
            vdn = jnp.pad(rw_v_down[l - 1], ((0, 0), (0, LANE - rw_v_down.shape[2]))).astype(BF16)
            vup = _pad_rows(rw_v_up[l - 1], LANE, 0).astype(BF16)
            rwkv_part = (functools.partial(_rwkv_kernel, True), _RWKV_STAGES, ("rw",), [v_first],
                         rw_consts + [_row(rw_v0[l - 1]), vdn, vup], (1,), [], rw_scratch)
            h, = _layer_call("layer", batch, tp, alpha, h, w_in_b[l], post_consts,
                             [rwkv_part, mlstm_part, pool_lru_part], l == depth - 1)

    return h.reshape(batch, seq, D_MODEL)
```

```python
import functools
import math

import jax
import jax.numpy as jnp
from jax import lax
from jax.experimental import pallas as pl
from jax.experimental.pallas import tpu as pltpu

D_MODEL = 1024
N_META = 16
GROUP_W = 256
HEAD = 64
N_H = 4
POOL_MAX_W = 16
CONV_W = 4
LRU_C = 8.0
LN_EPS = 1e-5
GN_EPS = 64e-5
NEG = -1e30
D_FF = 4096
RW_COLS = 896
ML_COLS_PAD = 1152
D_IN_PAD = 2816
LANE = 128
SUBLANE = 8

TB = 256
RW_CHUNK = 64
VMEM_LIMIT = 60 * 1024 * 1024

F32 = jnp.float32
BF16 = jnp.bfloat16

_NN = (((1,), (0,)), ((), ()))
_NT = (((1,), (1,)), ((), ()))
_TN = (((0,), (0,)), ((), ()))


def _dot(a, b, dims=_NN):
    return lax.dot_general(a, b, dims, preferred_element_type=F32)


def _bdot(a, b, dims=_NN):
    return _dot(a.astype(BF16), b.astype(BF16), dims)


def _split3(x):
    hi = x.astype(BF16)
    r1 = x - hi.astype(F32)
    mid = r1.astype(BF16)
    lo = (r1 - mid.astype(F32)).astype(BF16)
    return hi, mid, lo


def _dot_exact_lhs(e_bf16, x, dims=_NN):
    hi, mid, lo = _split3(x)
    return _dot(e_bf16, hi, dims) + _dot(e_bf16, mid, dims) + _dot(e_bf16, lo, dims)


def _softplus(x):
    return jnp.maximum(x, 0.0) + jnp.log(1.0 + jnp.exp(-jnp.abs(x)))


def _sigmoid(x):
    return 0.5 * jnp.tanh(0.5 * x) + 0.5


def _softplus_small(x):
    return jnp.maximum(x, 0.0) + jnp.log1p(jnp.exp(-jnp.abs(x)))


def _layer_norm(x, g, b):
    mu = jnp.mean(x, axis=-1, keepdims=True)
    xc = x - mu
    var = jnp.mean(xc * xc, axis=-1, keepdims=True)
    return xc * lax.rsqrt(var + LN_EPS) * g + b


def _head_masks(width=GROUP_W):
    lane = lax.broadcasted_iota(jnp.int32, (1, width), 1)
    return [(lane // HEAD) == h for h in range(N_H)]


def _head_ones():
    rr = lax.broadcasted_iota(jnp.int32, (GROUP_W, 1), 0) // HEAD
    cc = lax.broadcasted_iota(jnp.int32, (1, GROUP_W), 1) // HEAD
    same = rr == cc
    return same, jnp.where(same, 1.0, 0.0).astype(BF16)


def _head_sum(x, ones_b):
    return _dot(x.astype(BF16), ones_b)


def _head_norm(y, g, b, ones_b):
    mu = _head_sum(y, ones_b) * (1.0 / HEAD)
    yc = y - mu
    var = _head_sum(yc * yc, ones_b) * (1.0 / HEAD)
    return yc * lax.rsqrt(var + GN_EPS) * g + b


def _col(x, j):
    lane = lax.broadcasted_iota(jnp.int32, (1, x.shape[1]), 1)
    return jnp.sum(jnp.where(lane == j, x, 0.0), axis=1, keepdims=True)


def _valid_rows(t):
    row = t * TB + lax.broadcasted_iota(jnp.int32, (TB, 1), 0)
    return row, row >= (TB - N_META)


def _embed_ln_kernel(x_ref, meta_ref, g_ref, b_ref, o_ref):
    t = pl.program_id(0)
    batch = o_ref.shape[0]

    @pl.when(t == 0)
    def _():
        meta_n = _layer_norm(meta_ref[...], g_ref[...], b_ref[...])
        for b in range(batch):
            o_ref[b, 0:TB - N_META, :] = jnp.zeros((TB - N_META, D_MODEL), F32)
            o_ref[b, TB - N_META:TB, :] = meta_n

    @pl.when(t > 0)
    def _():
        for b in range(batch):
            o_ref[b] = _layer_norm(x_ref[b], g_ref[...], b_ref[...])


_SCAN_PAD = TB // 2
_POOL_OFF = SUBLANE + POOL_MAX_W
_INIT, _BODY = "init", "body"
_DONE = object()
_LEAD_STAGES = 2
_RWKV_STAGES, _MLSTM_STAGES, _POOL_LRU_STAGES, _INPROJ_STAGES, _POST_STAGES = 22, 16, 17, 11, 38


def _pool_lru_kernel(n_t, phase, pp_ref, plru_ref, poolw_ref, pscale_ref, convw_ref, convb_ref, wax_ref, bax_ref,
                     lam_ref, ypool_ref, ylru_ref, pext, pw2, pw4, pw8, cext, gbuf, s_a, s_u, hcar):
    t = pl.program_id(0) % n_t
    if phase == _INIT:
        @pl.when(t == 0)
        def _():
            pext[0:_POOL_OFF, :] = jnp.zeros((_POOL_OFF, GROUP_W), F32)
            for buf in (pw2, pw4, pw8):
                buf[0:SUBLANE, :] = jnp.zeros((SUBLANE, GROUP_W), F32)
            cext[0:SUBLANE, :] = jnp.zeros((SUBLANE, GROUP_W), F32)
            hcar[...] = jnp.zeros((1, GROUP_W), F32)
            s_a[0:_SCAN_PAD, :] = jnp.ones((_SCAN_PAD, GROUP_W), F32)
            s_u[0:_SCAN_PAD, :] = jnp.zeros((_SCAN_PAD, GROUP_W), F32)
        return

    row, valid = _valid_rows(t)

    u = jnp.where(valid, pp_ref[...], 0.0)
    lo_p, hi_p = SUBLANE, _POOL_OFF + TB
    pext[_POOL_OFF:hi_p, :] = u
    cext[SUBLANE:SUBLANE + TB, :] = jnp.where(valid, plru_ref[:, 0:GROUP_W], 0.0)
    gbuf[...] = plru_ref[:, GROUP_W:2 * GROUP_W]
    pw2[lo_p:hi_p, :] = pext[lo_p:hi_p, :] + pext[lo_p - 1:hi_p - 1, :]
    yield
    pw4[lo_p:hi_p, :] = pw2[lo_p:hi_p, :] + pw2[lo_p - 2:hi_p - 2, :]
    yield
    pw8[lo_p:hi_p, :] = pw4[lo_p:hi_p, :] + pw4[lo_p - 4:hi_p - 4, :]
    yield
    w16 = pw8[_POOL_OFF:hi_p, :] + pw8[_POOL_OFF - 8:hi_p - 8, :]
    grp = lax.broadcasted_iota(jnp.int32, (1, GROUP_W), 1) // HEAD
    acc = jnp.where(grp == 0, pw2[_POOL_OFF:hi_p, :],
                    jnp.where(grp == 1, pw4[_POOL_OFF:hi_p, :],
                              jnp.where(grp == 2, pw8[_POOL_OFF:hi_p, :], w16)))
    win = jnp.left_shift(2, grp)
    pos1 = (row - (TB - N_META) + 1).astype(F32)
    cnt = jnp.clip(pos1, 1.0, win.astype(F32))
    dlt = acc / cnt - u
    ypool_ref[...] = (_bdot(dlt, poolw_ref[...]) * pscale_ref[...]).astype(ypool_ref.dtype)
    pext[lo_p:_POOL_OFF, :] = pext[TB + lo_p:TB + _POOL_OFF, :]
    yield

    xc = jnp.zeros((TB, GROUP_W), F32) + convb_ref[...]
    for j in range(CONV_W):
        off = SUBLANE - (CONV_W - 1) + j
        xc = xc + convw_ref[j:j + 1, :] * cext[off:off + TB, :]
    cext[0:SUBLANE, :] = cext[TB:TB + SUBLANE, :]
    yield
    z = _bdot(xc, wax_ref[...]) + bax_ref[...]
    r = _sigmoid(z[:, 0:GROUP_W])
    i = _sigmoid(z[:, GROUP_W:2 * GROUP_W])
    yield
    log_a = (-LRU_C) * r * _softplus_small(-lam_ref[...])
    a = jnp.exp(log_a)
    uu = jnp.where(valid, jnp.sqrt(-jnp.tanh(log_a) * (a * a + 1.0)) * (i * xc), 0.0)
    lo, hi = _SCAN_PAD, _SCAN_PAD + TB
    s_a[lo:hi, :] = a
    s_u[lo:hi, :] = uu
    yield
    d = 1
    while d < TB:
        a0, u0 = s_a[lo:hi, :], s_u[lo:hi, :]
        a1, u1 = s_a[lo - d:hi - d, :], s_u[lo - d:hi - d, :]
        s_u[lo:hi, :] = a0 * u1 + u0
        s_a[lo:hi, :] = a0 * a1
        d *= 2
        yield
    h = s_u[lo:hi, :] + s_a[lo:hi, :] * hcar[...]
    hcar[...] = h[TB - 1:TB, :]
    yield
    gate = gbuf[...]
    gcube = gate * gate * gate
    gelu = 0.5 * gate * (1.0 + jnp.tanh(math.sqrt(2.0 / math.pi) * (gate + 0.044715 * gcube)))
    ylru_ref[...] = (h * gelu).astype(ylru_ref.dtype)


def _mlstm_kernel(n_t, phase, p_ref, ifb_ref, gng_ref, gnb_ref, y_ref, c_st, n_st, m_st):
    t = pl.program_id(0) % n_t
    if phase == _INIT:
        @pl.when(t == 0)
        def _():
            c_st[...] = jnp.zeros((GROUP_W, GROUP_W), F32)
            n_st[...] = jnp.zeros((1, GROUP_W), F32)
            m_st[...] = jnp.zeros((1, LANE), F32)
        return

    _, valid = _valid_rows(t)
    hms = _head_masks()
    q = jnp.where(valid, p_ref[:, 0:256], 0.0) * (HEAD ** -0.5)
    k = jnp.where(valid, p_ref[:, 256:512], 0.0)
    v = jnp.where(valid, p_ref[:, 512:768], 0.0)
    o = p_ref[:, 768:1024]
    g = p_ref[:, 1024:ML_COLS_PAD] + ifb_ref[...]
    lane = lax.broadcasted_iota(jnp.int32, (1, LANE), 1)
    is_i = lane < N_H
    is_f = jnp.logical_and(lane >= N_H, lane < 2 * N_H)
    logf = jnp.where(jnp.logical_and(is_f, valid), -_softplus(-g), 0.0)
    logi = jnp.where(is_i, jnp.where(valid, g, NEG), 0.0)

    tcol = lax.broadcasted_iota(jnp.int32, (TB, 1), 0)
    srow = lax.broadcasted_iota(jnp.int32, (1, TB), 1)
    causal = srow <= tcol
    tri = jnp.where(causal, 1.0, 0.0).astype(BF16)
    bcum = _dot_exact_lhs(tri, logf)
    srow8 = lax.broadcasted_iota(jnp.int32, (SUBLANE, LANE), 0)
    lane8 = lax.broadcasted_iota(jnp.int32, (SUBLANE, LANE), 1)
    psel = (jnp.where(lane8 == srow8, 1.0, 0.0) - jnp.where(lane8 == srow8 + N_H, 1.0, 0.0)).astype(BF16)
    yield
    zt = _dot_exact_lhs(psel, logi + bcum, _NT)
    yield

    m_prev = m_st[...]
    m_inter_t = bcum + m_prev
    kb, vb = k.astype(BF16), v.astype(BF16)
    qc = _bdot(q, c_st[...])
    qn = q * n_st[...]

    num = jnp.zeros((TB, GROUP_W), F32)
    wk_e = jnp.zeros((TB, GROUP_W), F32)
    sc_row = jnp.zeros((1, GROUP_W), F32)
    m_new_t = jnp.zeros((1, LANE), F32)
    for h in range(N_H):
        hm = hms[h]
        b_h = _col(bcum, N_H + h)
        dmat = jnp.where(causal, b_h + zt[h:h + 1, :], NEG)
        m_intra = jnp.max(dmat, axis=1, keepdims=True)
        m_inter = _col(m_inter_t, N_H + h)
        m_t = jnp.maximum(m_intra, m_inter)
        sc = _dot(jnp.where(hm, q, 0.0).astype(BF16), kb, _NT)
        pm = sc * jnp.exp(dmat - m_t)
        yield
        num_h = _dot(pm.astype(BF16), jnp.where(hm, v, 0.0).astype(BF16))
        w_int = jnp.exp(m_inter - m_t)
        den = (jnp.sum(pm, axis=1, keepdims=True)
               + w_int * jnp.sum(jnp.where(hm, qn, 0.0), axis=1, keepdims=True))
        denom = jnp.maximum(jnp.abs(den), jnp.exp(-m_t))
        num = num + (num_h + w_int * jnp.where(hm, qc, 0.0)) * (1.0 / denom)
        yield
        b_last = b_h[TB - 1:TB, :]
        g_loc = b_last + (_col(logi, h) - b_h)
        m_loc = jnp.max(g_loc, axis=0, keepdims=True)
        m_old = jnp.sum(jnp.where(lane == N_H + h, m_prev, 0.0), axis=1, keepdims=True)
        m_new = jnp.maximum(b_last + m_old, m_loc)
        wk_e = wk_e + jnp.where(hm, jnp.exp(g_loc - m_new), 0.0)
        sc_row = sc_row + jnp.where(hm, jnp.exp(b_last + m_old - m_new), 0.0)
        m_new_t = m_new_t + jnp.where(lane == N_H + h, m_new, 0.0)
        yield

    kw = k * wk_e
    blockdiag, ones_b = _head_ones()
    c_st[...] = sc_row * c_st[...] + jnp.where(blockdiag, _dot(kw.astype(BF16), vb, _TN), 0.0)
    n_st[...] = sc_row * n_st[...] + jnp.sum(kw, axis=0, keepdims=True)
    m_st[...] = m_new_t
    yield

    y_ref[...] = (_head_norm(num, gng_ref[...], gnb_ref[...], ones_b) * _sigmoid(o)).astype(y_ref.dtype)


def _rwkv_kernel(has_vfirst, n_t, phase, *refs):
    if has_vfirst:
        (p_ref, vf_ref, mu_ref, w0_ref, wup_ref, a0_ref, aup_ref, gup_ref, kk_ref, ka_ref, rk_ref, gng_ref,
         gnb_ref, v0_ref, vdn_ref, vup_ref, y_ref, pext, s_st) = refs
    else:
        (p_ref, mu_ref, w0_ref, wup_ref, a0_ref, aup_ref, gup_ref, kk_ref, ka_ref, rk_ref, gng_ref,
         gnb_ref, y_ref, vfo_ref, pext, s_st) = refs
    t = pl.program_id(0) % n_t
    if phase == _INIT:
        @pl.when(t == 0)
        def _():
            pext[0:SUBLANE, :] = jnp.zeros((SUBLANE, RW_COLS), F32)
            s_st[...] = jnp.zeros((GROUP_W, GROUP_W), F32)
        return

    _, valid = _valid_rows(t)
    blockdiag, bd_b = _head_ones()
    p = jnp.where(valid, p_ref[...], 0.0)
    pext[SUBLANE:SUBLANE + TB, :] = p
    prev = pext[SUBLANE - 1:SUBLANE - 1 + TB, :]
    pext[0:SUBLANE, :] = pext[TB:TB + SUBLANE, :]
    pm = p + (prev - p) * mu_ref[...]
    yield
    r = pm[:, 0:256]
    k = pm[:, 256:512]
    v = pm[:, 512:768]
    lora = pm[:, 768:RW_COLS]
    ld = (-math.exp(-0.5)) * _sigmoid(w0_ref[...] + _bdot(jnp.tanh(lora), wup_ref[...]))
    a = _sigmoid(a0_ref[...] + _bdot(lora, aup_ref[...]))
    g = _bdot(_sigmoid(lora), gup_ref[...])
    yield
    if has_vfirst:
        vmix = _sigmoid(v0_ref[...] + _bdot(_bdot(v, vdn_ref[...]), vup_ref[...]))
        v = v + (vf_ref[...] - v) * vmix
    else:
        vfo_ref[...] = v
    kk = k * kk_ref[...]
    kk = kk * lax.rsqrt(_head_sum(kk * kk, bd_b) + 1e-12)
    k = k * (1.0 + (a - 1.0) * ka_ref[...])
    bvec = kk * a
    yield

    tcol = lax.broadcasted_iota(jnp.int32, (TB, 1), 0)
    srow = lax.broadcasted_iota(jnp.int32, (1, TB), 1)
    cum_mask = jnp.logical_and((tcol // RW_CHUNK) == (srow // RW_CHUNK), srow <= tcol)
    lc = _dot_exact_lhs(jnp.where(cum_mask, 1.0, 0.0).astype(BF16), ld)
    n_chunks = TB // RW_CHUNK
    chunks = range(n_chunks)
    lc_last = [lc[(c + 1) * RW_CHUNK - 1:(c + 1) * RW_CHUNK, :] for c in chunks]
    lc_end = jnp.concatenate([jnp.broadcast_to(x, (RW_CHUNK, GROUP_W)) for x in lc_last], axis=0)
    yield
    e_neg = jnp.exp(-lc)
    kap = (kk * jnp.exp(lc - ld)).astype(BF16)
    kt = (k * e_neg).astype(BF16)
    bt = (bvec * e_neg).astype(BF16)
    rt = r * jnp.exp(lc)
    rtb = rt.astype(BF16)
    vb = v.astype(BF16)
    e_end = jnp.exp(lc_end - lc)
    kbar = (k * e_end).astype(BF16)
    bbar = (bvec * e_end).astype(BF16)
    yield

    def bd(xb):
        return jnp.concatenate([xb] * N_H, axis=0) * bd_b

    tloc = lax.broadcasted_iota(jnp.int32, (RW_CHUNK, 1), 0)
    sloc = lax.broadcasted_iota(jnp.int32, (1, GROUP_W), 1) % RW_CHUNK
    strict = sloc < tloc
    incl = sloc <= tloc
    eye_pk = jnp.where(sloc == tloc, 1.0, 0.0)
    sls = [slice(c * RW_CHUNK, (c + 1) * RW_CHUNK) for c in chunks]

    lhs_kr = [jnp.concatenate([kap[sl], rtb[sl]], axis=0) for sl in sls]
    s_k = [_dot(lhs_kr[c], bd(kt[sls[c]]), _NT) for c in chunks]
    s_b = [_dot(lhs_kr[c], bd(bt[sls[c]]), _NT) for c in chunks]
    yield
    nmat = [jnp.where(strict, -s_b[c][0:RW_CHUNK], 0.0) for c in chunks]
    zs = [eye_pk + nmat[c] for c in chunks]
    ypow = [nmat[c].astype(BF16) for c in chunks]
    ypow = [_dot(ypow[c], bd(ypow[c])).astype(BF16) for c in chunks]
    n_iter = int(math.log2(RW_CHUNK)) - 1
    yield
    for it in range(n_iter):
        rhs = [bd(ypow[c]) for c in chunks]
        if it + 1 < n_iter:
            res = [_dot(jnp.concatenate([ypow[c], zs[c].astype(BF16)], axis=0), rhs[c]) for c in chunks]
            ypow = [res[c][0:RW_CHUNK].astype(BF16) for c in chunks]
            zs = [zs[c] + res[c][RW_CHUNK:2 * RW_CHUNK] for c in chunks]
        else:
            zs = [zs[c] + _dot(zs[c].astype(BF16), rhs[c]) for c in chunks]
        yield
    tinv = [zs[c].astype(BF16) for c in chunks]
    lhs_v = [jnp.concatenate([jnp.where(strict, s_k[c][0:RW_CHUNK], 0.0),
                              jnp.where(incl, s_k[c][RW_CHUNK:2 * RW_CHUNK], 0.0)], axis=0).astype(BF16)
             for c in chunks]
    res_v = [_dot(lhs_v[c], bd(vb[sls[c]])) for c in chunks]
    yield
    p1 = [_dot(tinv[c], bd(kap[sls[c]])) for c in chunks]
    p2 = [_dot(tinv[c], bd(res_v[c][0:RW_CHUNK].astype(BF16))) for c in chunks]
    yield
    a_rb = [jnp.where(incl, s_b[c][RW_CHUNK:2 * RW_CHUNK], 0.0).astype(BF16) for c in chunks]
    y0 = [res_v[c][RW_CHUNK:2 * RW_CHUNK] - _dot(a_rb[c], bd(p2[c].astype(BF16))) for c in chunks]
    qq = [rt[sls[c]] - _dot(a_rb[c], bd(p1[c].astype(BF16))) for c in chunks]
    yield

    s_val = s_st[...]
    ys = []
    for c in chunks:
        sl = sls[c]
        lhs = jnp.concatenate([p1[c], qq[c]], axis=0)
        res = _bdot(lhs, s_val, _NT)
        u_c = res[0:RW_CHUNK] + p2[c]
        ys.append(y0[c] + res[RW_CHUNK:2 * RW_CHUNK])
        lt = jnp.concatenate([vb[sl], (-u_c).astype(BF16)], axis=0)
        rt2 = jnp.concatenate([kbar[sl], bbar[sl]], axis=0)
        s_val = s_val * jnp.exp(lc_last[c]) + jnp.where(blockdiag, _dot(lt, rt2, _TN), 0.0)
        yield
    s_st[...] = s_val
    y = jnp.concatenate(ys, axis=0)

    yn = _head_norm(y, gng_ref[...], gnb_ref[...], bd_b)
    yield
    bonus = _head_sum(r * k * rk_ref[...], bd_b) * v
    y_ref[...] = ((yn + bonus) * g).astype(y_ref.dtype)


_P_SEGMENTS = (("pool", 0, 256), ("rw", 256, 1152), ("lru", 1152, 1664), ("ml", 1664, D_IN_PAD))


def _inproj_stages(h_ref, w_ref, p_refs):
    hb = h_ref[...].astype(BF16)
    for name, lo, hi in _P_SEGMENTS:
        for c0 in range(lo, hi, GROUP_W):
            c1 = min(c0 + GROUP_W, hi)
            p_refs[name][:, c0 - lo:c1 - lo] = _dot(hb, w_ref[:, c0:c1])
            yield


def _post_stages(alpha, y_s, h_ref, wo_ref, g1_ref, b1_ref, w1_ref, w2_ref, g2_ref, b2_ref, o_ref,
                 h1_s, hb_s, zb_s, acc_s):
    slabs = [slice(n * GROUP_W, (n + 1) * GROUP_W) for n in range(D_MODEL // GROUP_W)]
    for ns in slabs:
        h1_s[:, ns] = alpha * h_ref[:, ns] + _dot(y_s[...], wo_ref[:, ns])
        yield
    h1 = _layer_norm(h1_s[...], g1_ref[...], b1_ref[...])
    h1_s[...] = h1
    hb_s[...] = h1.astype(BF16)
    yield
    for j in range(D_FF // D_MODEL):
        for ns in slabs:
            z = jnp.maximum(_dot(hb_s[...], w1_ref[:, j * D_MODEL + ns.start:j * D_MODEL + ns.stop]), 0.0)
            zb_s[:, ns] = (z * z).astype(BF16)
            yield
        for ns in slabs:
            upd = _dot(zb_s[...], w2_ref[j * D_MODEL:(j + 1) * D_MODEL, ns])
            acc_s[:, ns] = upd if j == 0 else acc_s[:, ns] + upd
            yield
    o_ref[...] = _layer_norm(alpha * h1_s[...] + acc_s[...], g2_ref[...], b2_ref[...])


def _fused_layer_kernel(parts, alpha, *refs):
    h_next_ref, h_first_ref, h_prev_ref = refs[:3]
    w_ref = refs[3].at[0]
    wo_ref, g1_ref, b1_ref, w1_ref, w2_ref, g2_ref, b2_ref = refs[4:11]
    post_consts = (wo_ref.at[0], g1_ref, b1_ref, w1_ref.at[0], w2_ref.at[0], g2_ref, b2_ref)
    refs = refs[11:]
    n_in = sum(p[3] + p[4] for p in parts)
    n_out = 1 + sum(p[6] for p in parts)
    ins, outs, scr = refs[:n_in], refs[n_in:n_in + n_out], refs[n_in + n_out:]
    p_refs = {name: r for (name, _, _), r in zip(_P_SEGMENTS, scr)}
    y_s = scr[len(_P_SEGMENTS)]
    post_scr = scr[len(_P_SEGMENTS) + 1:len(_P_SEGMENTS) + 5]
    scr = scr[len(_P_SEGMENTS) + 5:]
    h_out_ref, outs = outs[0], outs[1:]

    @pl.when(pl.program_id(0) == 0)
    def _():
        y_s[...] = jnp.zeros(y_s.shape, y_s.dtype)
        for _ in _inproj_stages(h_first_ref.at[0], w_ref, p_refs):
            pass

    for phase in (_INIT, _BODY):
        gens, totals = [], []
        i = o = s = 0
        for fn, n_stages, pnames, nt, nc, y_slots, no, ns in parts:
            args = [p_refs[n] for n in pnames] + [r.at[0] for r in ins[i:i + nt]]
            args += list(ins[i + nt:i + nt + nc])
            args += [y_s.at[:, k * GROUP_W:(k + 1) * GROUP_W] for k in y_slots]
            args += [r.at[0] for r in outs[o:o + no]] + list(scr[s:s + ns])
            gens.append(fn(phase, *args))
            totals.append(n_stages)
            i, o, s = i + nt + nc, o + no, s + ns
        if phase == _BODY:
            gens.append(_inproj_stages(h_next_ref.at[0], w_ref, p_refs))
            totals.append(_INPROJ_STAGES)
            gens.append(_post_stages(alpha, y_s, h_prev_ref.at[0], *post_consts, h_out_ref.at[0], *post_scr))
            totals.append(_POST_STAGES)
        done = [0] * len(gens)

        def advance(j):
            if next(gens[j], _DONE) is _DONE:
                done[j] = None
            else:
                done[j] += 1

        for _ in range(_LEAD_STAGES):
            advance(0)
        for j in range(len(gens)):
            if done[j] == 0:
                advance(j)
        while any(d is not None for d in done):
            j = min((d / totals[k], k) for k, d in enumerate(done) if d is not None)[1]
            advance(j)


def _layer_call(name, layer, batch, tp, alpha, h, w_in_b, post_consts, parts, drop_front):
    n_t = tp // TB
    n_blk = batch * n_t

    def tspec(width, index):
        return pl.BlockSpec((1, TB, width), index)

    def cspec(shape):
        return pl.BlockSpec(shape, lambda s: (0,) * len(shape), pipeline_mode=pl.Buffered(1))

    def wspec(a):
        if a.ndim == 3:
            return pl.BlockSpec((1,) + a.shape[1:], lambda s: (layer, 0, 0), pipeline_mode=pl.Buffered(1))
        return cspec(a.shape)

    def out_index(s):
        blk = jnp.maximum(s - 1, 0)
        if not drop_front:
            return (blk, 0, 0)
        return ((blk // n_t) * (n_t - 1) + jnp.maximum(blk % n_t - 1, 0), 0, 0)

    in_specs = [tspec(D_MODEL, lambda s: (jnp.minimum(s + 1, n_blk - 1), 0, 0)),
                tspec(D_MODEL, lambda s: (0, 0, 0)),
                tspec(D_MODEL, lambda s: (jnp.maximum(s - 1, 0), 0, 0)),
                wspec(w_in_b)] + [wspec(a) for a in post_consts]
    operands = [h, h, h, w_in_b] + list(post_consts)
    out_shape = [jax.ShapeDtypeStruct((batch * (n_t - 1) if drop_front else n_blk, TB, D_MODEL), F32)]
    out_specs = [tspec(D_MODEL, out_index)]
    scratch = [pltpu.VMEM((TB, hi - lo), F32) for _, lo, hi in _P_SEGMENTS]
    scratch += [pltpu.VMEM((TB, D_MODEL), BF16), pltpu.VMEM((TB, D_MODEL), F32), pltpu.VMEM((TB, D_MODEL), BF16),
                pltpu.VMEM((TB, D_MODEL), BF16), pltpu.VMEM((TB, D_MODEL), F32)]
    sig = []
    for fn, n_stages, pnames, time_ins, const_ins, y_slots, extra_widths, scr in parts:
        in_specs += [tspec(a.shape[2], lambda s: (jnp.minimum(s, n_blk - 1), 0, 0)) for a in time_ins]
        in_specs += [cspec(a.shape) for a in const_ins]
        operands += list(time_ins) + list(const_ins)
        out_shape += [jax.ShapeDtypeStruct((n_blk + 1, TB, w), F32) for w in extra_widths]
        out_specs += [tspec(w, lambda s: (s, 0, 0)) for w in extra_widths]
        scratch += [pltpu.VMEM(tuple(shape), F32) for shape in scr]
        sig.append((functools.partial(fn, n_t), n_stages, tuple(pnames), len(time_ins), len(const_ins),
                    tuple(y_slots), len(extra_widths), len(scr)))
    return pl.pallas_call(
        functools.partial(_fused_layer_kernel, tuple(sig), alpha),
        grid=(n_blk + 1,),
        in_specs=in_specs,
        out_specs=out_specs,
        out_shape=out_shape,
        scratch_shapes=scratch,
        compiler_params=pltpu.CompilerParams(dimension_semantics=("arbitrary",),
                                             vmem_limit_bytes=VMEM_LIMIT),
        name=name,
    )(*operands)


def _block_diag(w):
    eye = jnp.eye(N_H, dtype=w.dtype)
    return jnp.einsum('gcd,gh->gchd', w, eye).reshape(GROUP_W, GROUP_W)


def _pad_rows(w, total, offset):
    return jnp.zeros((total, w.shape[1]), w.dtype).at[offset:offset + w.shape[0]].set(w)


def _row(v):
    return v.reshape(1, -1)


@jax.jit
def kernel(x, meta, emb_ln_g, emb_ln_b, w_in, w_out, pool_w, pool_scale, rw_mu, rw_w0, rw_w_up, rw_a0, rw_a_up, rw_g_up, rw_k_k, rw_k_a, rw_r_k, rw_gn_g, rw_gn_b, rw_v0, rw_v_down, rw_v_up, lru_conv_w, lru_conv_b, lru_ga_w, lru_ga_b, lru_gx_w, lru_gx_b, lru_lambda, ml_if_b, ml_gn_g, ml_gn_b, ln1_g, ln1_b, ln2_g, ln2_b, mlp_w1, mlp_w2):
    batch, seq, _ = x.shape
    depth = w_in.shape[0]
    assert seq % TB == 0 and x.shape[2] == D_MODEL
    tp = TB + seq
    alpha = (2 * depth) ** 0.25

    h = pl.pallas_call(
        _embed_ln_kernel,
        grid=(tp // TB,),
        in_specs=[pl.BlockSpec((batch, TB, D_MODEL), lambda t: (0, jnp.maximum(t - 1, 0), 0)),
                  pl.BlockSpec((N_META, D_MODEL), lambda t: (0, 0)),
                  pl.BlockSpec((1, D_MODEL), lambda t: (0, 0)),
                  pl.BlockSpec((1, D_MODEL), lambda t: (0, 0))],
        out_specs=pl.BlockSpec((batch, TB, D_MODEL), lambda t: (0, t, 0)),
        out_shape=jax.ShapeDtypeStruct((batch, tp, D_MODEL), F32),
        compiler_params=pltpu.CompilerParams(dimension_semantics=("arbitrary",)),
        name="embed_ln",
    )(x, meta.astype(x.dtype), _row(emb_ln_g), _row(emb_ln_b)).reshape(batch * (tp // TB), TB, D_MODEL)

    w_in_b = jnp.pad(w_in, ((0, 0), (0, 0), (0, D_IN_PAD - w_in.shape[2]))).astype(BF16)
    w_out_b = w_out.astype(BF16)
    w1_b = mlp_w1.astype(BF16)
    w2_b = mlp_w2.astype(BF16)

    v_first = None
    for l in range(depth):
        wax = jnp.concatenate([_block_diag(lru_ga_w[l]), _block_diag(lru_gx_w[l])], axis=1).astype(BF16)
        bax = jnp.concatenate([lru_ga_b[l], lru_gx_b[l]]).reshape(1, 2 * GROUP_W)
        pool_lru_part = (
            _pool_lru_kernel, _POOL_LRU_STAGES, ("pool", "lru"), [],
            [_block_diag(pool_w[l]).astype(BF16), _row(pool_scale[l]), lru_conv_w[l], _row(lru_conv_b[l]),
             wax, bax, _row(lru_lambda[l])],
            (0, 2), [],
            [(TB + _POOL_OFF, GROUP_W)] * 4 + [(TB + SUBLANE, GROUP_W), (TB, GROUP_W),
                                               (TB + _SCAN_PAD, GROUP_W), (TB + _SCAN_PAD, GROUP_W),
                                               (1, GROUP_W)])

        ifb = jnp.zeros((1, LANE), F32).at[0, 0:2 * N_H].set(ml_if_b[l])
        mlstm_part = (
            _mlstm_kernel, _MLSTM_STAGES, ("ml",), [], [ifb, _row(ml_gn_g[l]), _row(ml_gn_b[l])], (3,), [],
            [(GROUP_W, GROUP_W), (1, GROUP_W), (1, LANE)])

        wup = _pad_rows(rw_w_up[l], LANE, 0).astype(BF16)
        aup = _pad_rows(rw_a_up[l], LANE, 32).astype(BF16)
        gup = _pad_rows(rw_g_up[l], LANE, 64).astype(BF16)
        rw_consts = [_row(rw_mu[l]), _row(rw_w0[l]), wup, _row(rw_a0[l]), aup, gup, _row(rw_k_k[l]),
                     _row(rw_k_a[l]), _row(rw_r_k[l]), _row(rw_gn_g[l]), _row(rw_gn_b[l])]
        rw_scratch = [(TB + SUBLANE, RW_COLS), (GROUP_W, GROUP_W)]
        post_consts = [w_out_b, _row(ln1_g[l]), _row(ln1_b[l]), w1_b, w2_b, _row(ln2_g[l]), _row(ln2_b[l])]
        if l == 0:
            rwkv_part = (functools.partial(_rwkv_kernel, False), _RWKV_STAGES, ("rw",), [], rw_consts,
                         (1,), [GROUP_W], rw_scratch)
            h, v_first = _layer_call("layer_first", l, batch, tp, alpha, h, w_in_b, post_consts,
                                     [rwkv_part, mlstm_part, pool_lru_part], l == depth - 1)
        else:
            vdn = jnp.pad(rw_v_down[l - 1], ((0, 0), (0, LANE - rw_v_down.shape[2]))).astype(BF16)
            vup = _pad_rows(rw_v_up[l - 1], LANE, 0).astype(BF16)
            rwkv_part = (functools.partial(_rwkv_kernel, True), _RWKV_STAGES, ("rw",), [v_first],
                         rw_consts + [_row(rw_v0[l - 1]), vdn, vup], (1,), [], rw_scratch)
            h, = _layer_call("layer", l, batch, tp, alpha, h, w_in_b, post_consts,
                             [rwkv_part, mlstm_part, pool_lru_part], l == depth - 1)

    return h.reshape(batch, seq, D_MODEL)
```

```python
import functools
import math

import jax
import jax.numpy as jnp
from jax import lax
from jax.experimental import pallas as pl
from jax.experimental.pallas import tpu as pltpu

D_MODEL = 1024
N_META = 16
GROUP_W = 256
HEAD = 64
N_H = 4
POOL_MAX_W = 16
CONV_W = 4
LRU_C = 8.0
LN_EPS = 1e-5
GN_EPS = 64e-5
NEG = -1e30
D_FF = 4096
RW_COLS = 896
ML_COLS_PAD = 1152
D_IN_PAD = 2816
LANE = 128
SUBLANE = 8

TB = 256
RW_CHUNK = 64
VMEM_LIMIT = 60 * 1024 * 1024

F32 = jnp.float32
BF16 = jnp.bfloat16

_NN = (((1,), (0,)), ((), ()))
_NT = (((1,), (1,)), ((), ()))
_TN = (((0,), (0,)), ((), ()))


def _dot(a, b, dims=_NN):
    return lax.dot_general(a, b, dims, preferred_element_type=F32)


def _bdot(a, b, dims=_NN):
    return _dot(a.astype(BF16), b.astype(BF16), dims)


def _split3(x):
    hi = x.astype(BF16)
    r1 = x - hi.astype(F32)
    mid = r1.astype(BF16)
    lo = (r1 - mid.astype(F32)).astype(BF16)
    return hi, mid, lo


def _dot_exact_lhs(e_bf16, x, dims=_NN):
    hi, mid, lo = _split3(x)
    return _dot(e_bf16, hi, dims) + _dot(e_bf16, mid, dims) + _dot(e_bf16, lo, dims)


def _softplus(x):
    return jnp.maximum(x, 0.0) + jnp.log(1.0 + jnp.exp(-jnp.abs(x)))


def _sigmoid(x):
    return 0.5 * jnp.tanh(0.5 * x) + 0.5


def _softplus_small(x):
    return jnp.maximum(x, 0.0) + jnp.log1p(jnp.exp(-jnp.abs(x)))


def _layer_norm(x, g, b):
    mu = jnp.mean(x, axis=-1, keepdims=True)
    xc = x - mu
    var = jnp.mean(xc * xc, axis=-1, keepdims=True)
    return xc * lax.rsqrt(var + LN_EPS) * g + b


def _head_masks(width=GROUP_W):
    lane = lax.broadcasted_iota(jnp.int32, (1, width), 1)
    return [(lane // HEAD) == h for h in range(N_H)]


def _head_ones():
    rr = lax.broadcasted_iota(jnp.int32, (GROUP_W, 1), 0) // HEAD
    cc = lax.broadcasted_iota(jnp.int32, (1, GROUP_W), 1) // HEAD
    same = rr == cc
    return same, jnp.where(same, 1.0, 0.0).astype(BF16)


def _head_sum(x, ones_b):
    return _dot(x.astype(BF16), ones_b)


def _head_norm(y, g, b, ones_b):
    mu = _head_sum(y, ones_b) * (1.0 / HEAD)
    yc = y - mu
    var = _head_sum(yc * yc, ones_b) * (1.0 / HEAD)
    return yc * lax.rsqrt(var + GN_EPS) * g + b


def _col(x, j):
    lane = lax.broadcasted_iota(jnp.int32, (1, x.shape[1]), 1)
    return jnp.sum(jnp.where(lane == j, x, 0.0), axis=1, keepdims=True)


def _valid_rows(t):
    row = t * TB + lax.broadcasted_iota(jnp.int32, (TB, 1), 0)
    return row, row >= (TB - N_META)


def _embed_ln_kernel(x_ref, meta_ref, g_ref, b_ref, o_ref):
    t = pl.program_id(0)
    batch = o_ref.shape[0]

    @pl.when(t == 0)
    def _():
        meta_n = _layer_norm(meta_ref[...], g_ref[...], b_ref[...])
        for b in range(batch):
            o_ref[b, 0:TB - N_META, :] = jnp.zeros((TB - N_META, D_MODEL), F32)
            o_ref[b, TB - N_META:TB, :] = meta_n

    @pl.when(t > 0)
    def _():
        for b in range(batch):
            o_ref[b] = _layer_norm(x_ref[b], g_ref[...], b_ref[...])


_SCAN_PAD = TB // 2
_POOL_OFF = SUBLANE + POOL_MAX_W
_INIT, _BODY = "init", "body"
_DONE = object()
_LEAD_STAGES = 2
_RWKV_STAGES, _MLSTM_STAGES, _POOL_LRU_STAGES, _INPROJ_STAGES, _POST_STAGES = 22, 16, 17, 11, 38


def _pool_lru_kernel(n_t, phase, pp_ref, plru_ref, poolw_ref, pscale_ref, convw_ref, convb_ref, wax_ref, bax_ref,
                     lam_ref, ypool_ref, ylru_ref, pext, pw2, pw4, pw8, cext, gbuf, s_a, s_u, hcar):
    t = pl.program_id(0) % n_t
    if phase == _INIT:
        @pl.when(t == 0)
        def _():
            pext[0:_POOL_OFF, :] = jnp.zeros((_POOL_OFF, GROUP_W), F32)
            for buf in (pw2, pw4, pw8):
                buf[0:SUBLANE, :] = jnp.zeros((SUBLANE, GROUP_W), F32)
            cext[0:SUBLANE, :] = jnp.zeros((SUBLANE, GROUP_W), F32)
            hcar[...] = jnp.zeros((1, GROUP_W), F32)
            s_a[0:_SCAN_PAD, :] = jnp.ones((_SCAN_PAD, GROUP_W), F32)
            s_u[0:_SCAN_PAD, :] = jnp.zeros((_SCAN_PAD, GROUP_W), F32)
        return

    row, valid = _valid_rows(t)

    u = jnp.where(valid, pp_ref[...], 0.0)
    lo_p, hi_p = SUBLANE, _POOL_OFF + TB
    pext[_POOL_OFF:hi_p, :] = u
    cext[SUBLANE:SUBLANE + TB, :] = jnp.where(valid, plru_ref[:, 0:GROUP_W], 0.0)
    gbuf[...] = plru_ref[:, GROUP_W:2 * GROUP_W]
    pw2[lo_p:hi_p, :] = pext[lo_p:hi_p, :] + pext[lo_p - 1:hi_p - 1, :]
    yield
    pw4[lo_p:hi_p, :] = pw2[lo_p:hi_p, :] + pw2[lo_p - 2:hi_p - 2, :]
    yield
    pw8[lo_p:hi_p, :] = pw4[lo_p:hi_p, :] + pw4[lo_p - 4:hi_p - 4, :]
    yield
    w16 = pw8[_POOL_OFF:hi_p, :] + pw8[_POOL_OFF - 8:hi_p - 8, :]
    grp = lax.broadcasted_iota(jnp.int32, (1, GROUP_W), 1) // HEAD
    acc = jnp.where(grp == 0, pw2[_POOL_OFF:hi_p, :],
                    jnp.where(grp == 1, pw4[_POOL_OFF:hi_p, :],
                              jnp.where(grp == 2, pw8[_POOL_OFF:hi_p, :], w16)))
    win = jnp.left_shift(2, grp)
    pos1 = (row - (TB - N_META) + 1).astype(F32)
    cnt = jnp.clip(pos1, 1.0, win.astype(F32))
    dlt = acc / cnt - u
    ypool_ref[...] = (_bdot(dlt, poolw_ref[...]) * pscale_ref[...]).astype(ypool_ref.dtype)
    pext[lo_p:_POOL_OFF, :] = pext[TB + lo_p:TB + _POOL_OFF, :]
    yield

    xc = jnp.zeros((TB, GROUP_W), F32) + convb_ref[...]
    for j in range(CONV_W):
        off = SUBLANE - (CONV_W - 1) + j
        xc = xc + convw_ref[j:j + 1, :] * cext[off:off + TB, :]
    cext[0:SUBLANE, :] = cext[TB:TB + SUBLANE, :]
    yield
    z = _bdot(xc, wax_ref[...]) + bax_ref[...]
    r = _sigmoid(z[:, 0:GROUP_W])
    i = _sigmoid(z[:, GROUP_W:2 * GROUP_W])
    yield
    log_a = (-LRU_C) * r * _softplus_small(-lam_ref[...])
    a = jnp.exp(log_a)
    uu = jnp.where(valid, jnp.sqrt(-jnp.tanh(log_a) * (a * a + 1.0)) * (i * xc), 0.0)
    lo, hi = _SCAN_PAD, _SCAN_PAD + TB
    s_a[lo:hi, :] = a
    s_u[lo:hi, :] = uu
    yield
    d = 1
    while d < TB:
        a0, u0 = s_a[lo:hi, :], s_u[lo:hi, :]
        a1, u1 = s_a[lo - d:hi - d, :], s_u[lo - d:hi - d, :]
        s_u[lo:hi, :] = a0 * u1 + u0
        s_a[lo:hi, :] = a0 * a1
        d *= 2
        yield
    h = s_u[lo:hi, :] + s_a[lo:hi, :] * hcar[...]
    hcar[...] = h[TB - 1:TB, :]
    yield
    gate = gbuf[...]
    gcube = gate * gate * gate
    gelu = 0.5 * gate * (1.0 + jnp.tanh(math.sqrt(2.0 / math.pi) * (gate + 0.044715 * gcube)))
    ylru_ref[...] = (h * gelu).astype(ylru_ref.dtype)


def _mlstm_kernel(n_t, phase, p_ref, ifb_ref, gng_ref, gnb_ref, y_ref, c_st, n_st, m_st):
    t = pl.program_id(0) % n_t
    if phase == _INIT:
        @pl.when(t == 0)
        def _():
            c_st[...] = jnp.zeros((GROUP_W, GROUP_W), F32)
            n_st[...] = jnp.zeros((1, GROUP_W), F32)
            m_st[...] = jnp.zeros((1, LANE), F32)
        return

    _, valid = _valid_rows(t)
    hms = _head_masks()
    q = jnp.where(valid, p_ref[:, 0:256], 0.0) * (HEAD ** -0.5)
    k = jnp.where(valid, p_ref[:, 256:512], 0.0)
    v = jnp.where(valid, p_ref[:, 512:768], 0.0)
    o = p_ref[:, 768:1024]
    g = p_ref[:, 1024:ML_COLS_PAD] + ifb_ref[...]
    lane = lax.broadcasted_iota(jnp.int32, (1, LANE), 1)
    is_i = lane < N_H
    is_f = jnp.logical_and(lane >= N_H, lane < 2 * N_H)
    logf = jnp.where(jnp.logical_and(is_f, valid), -_softplus(-g), 0.0)
    logi = jnp.where(is_i, jnp.where(valid, g, NEG), 0.0)

    tcol = lax.broadcasted_iota(jnp.int32, (TB, 1), 0)
    srow = lax.broadcasted_iota(jnp.int32, (1, TB), 1)
    causal = srow <= tcol
    tri = jnp.where(causal, 1.0, 0.0).astype(BF16)
    bcum = _dot_exact_lhs(tri, logf)
    srow8 = lax.broadcasted_iota(jnp.int32, (SUBLANE, LANE), 0)
    lane8 = lax.broadcasted_iota(jnp.int32, (SUBLANE, LANE), 1)
    psel = (jnp.where(lane8 == srow8, 1.0, 0.0) - jnp.where(lane8 == srow8 + N_H, 1.0, 0.0)).astype(BF16)
    yield
    zt = _dot_exact_lhs(psel, logi + bcum, _NT)
    yield

    m_prev = m_st[...]
    m_inter_t = bcum + m_prev
    kb, vb = k.astype(BF16), v.astype(BF16)
    qc = _bdot(q, c_st[...])
    qn = q * n_st[...]

    num = jnp.zeros((TB, GROUP_W), F32)
    wk_e = jnp.zeros((TB, GROUP_W), F32)
    sc_row = jnp.zeros((1, GROUP_W), F32)
    m_new_t = jnp.zeros((1, LANE), F32)
    for h in range(N_H):
        hm = hms[h]
        b_h = _col(bcum, N_H + h)
        dmat = jnp.where(causal, b_h + zt[h:h + 1, :], NEG)
        m_intra = jnp.max(dmat, axis=1, keepdims=True)
        m_inter = _col(m_inter_t, N_H + h)
        m_t = jnp.maximum(m_intra, m_inter)
        sc = _dot(jnp.where(hm, q, 0.0).astype(BF16), kb, _NT)
        pm = sc * jnp.exp(dmat - m_t)
        yield
        num_h = _dot(pm.astype(BF16), jnp.where(hm, v, 0.0).astype(BF16))
        w_int = jnp.exp(m_inter - m_t)
        den = (jnp.sum(pm, axis=1, keepdims=True)
               + w_int * jnp.sum(jnp.where(hm, qn, 0.0), axis=1, keepdims=True))
        denom = jnp.maximum(jnp.abs(den), jnp.exp(-m_t))
        num = num + (num_h + w_int * jnp.where(hm, qc, 0.0)) * (1.0 / denom)
        yield
        b_last = b_h[TB - 1:TB, :]
        g_loc = b_last + (_col(logi, h) - b_h)
        m_loc = jnp.max(g_loc, axis=0, keepdims=True)
        m_old = jnp.sum(jnp.where(lane == N_H + h, m_prev, 0.0), axis=1, keepdims=True)
        m_new = jnp.maximum(b_last + m_old, m_loc)
        wk_e = wk_e + jnp.where(hm, jnp.exp(g_loc - m_new), 0.0)
        sc_row = sc_row + jnp.where(hm, jnp.exp(b_last + m_old - m_new), 0.0)
        m_new_t = m_new_t + jnp.where(lane == N_H + h, m_new, 0.0)
        yield

    kw = k * wk_e
    blockdiag, ones_b = _head_ones()
    c_st[...] = sc_row * c_st[...] + jnp.where(blockdiag, _dot(kw.astype(BF16), vb, _TN), 0.0)
    n_st[...] = sc_row * n_st[...] + jnp.sum(kw, axis=0, keepdims=True)
    m_st[...] = m_new_t
    yield

    y_ref[...] = (_head_norm(num, gng_ref[...], gnb_ref[...], ones_b) * _sigmoid(o)).astype(y_ref.dtype)


def _rwkv_kernel(has_vfirst, n_t, phase, *refs):
    if has_vfirst:
        (p_ref, vf_ref, mu_ref, w0_ref, wup_ref, a0_ref, aup_ref, gup_ref, kk_ref, ka_ref, rk_ref, gng_ref,
         gnb_ref, v0_ref, vdn_ref, vup_ref, y_ref, pext, s_st) = refs
    else:
        (p_ref, mu_ref, w0_ref, wup_ref, a0_ref, aup_ref, gup_ref, kk_ref, ka_ref, rk_ref, gng_ref,
         gnb_ref, y_ref, vfo_ref, pext, s_st) = refs
    t = pl.program_id(0) % n_t
    if phase == _INIT:
        @pl.when(t == 0)
        def _():
            pext[0:SUBLANE, :] = jnp.zeros((SUBLANE, RW_COLS), F32)
            s_st[...] = jnp.zeros((GROUP_W, GROUP_W), F32)
        return

    _, valid = _valid_rows(t)
    blockdiag, bd_b = _head_ones()
    p = jnp.where(valid, p_ref[...], 0.0)
    pext[SUBLANE:SUBLANE + TB, :] = p
    prev = pext[SUBLANE - 1:SUBLANE - 1 + TB, :]
    pext[0:SUBLANE, :] = pext[TB:TB + SUBLANE, :]
    pm = p + (prev - p) * mu_ref[...]
    yield
    r = pm[:, 0:256]
    k = pm[:, 256:512]
    v = pm[:, 512:768]
    lora = pm[:, 768:RW_COLS]
    ld = (-math.exp(-0.5)) * _sigmoid(w0_ref[...] + _bdot(jnp.tanh(lora), wup_ref[...]))
    a = _sigmoid(a0_ref[...] + _bdot(lora, aup_ref[...]))
    g = _bdot(_sigmoid(lora), gup_ref[...])
    yield
    if has_vfirst:
        vmix = _sigmoid(v0_ref[...] + _bdot(_bdot(v, vdn_ref[...]), vup_ref[...]))
        v = v + (vf_ref[...] - v) * vmix
    else:
        vfo_ref[...] = v
    kk = k * kk_ref[...]
    kk = kk * lax.rsqrt(_head_sum(kk * kk, bd_b) + 1e-12)
    k = k * (1.0 + (a - 1.0) * ka_ref[...])
    bvec = kk * a
    yield

    tcol = lax.broadcasted_iota(jnp.int32, (TB, 1), 0)
    srow = lax.broadcasted_iota(jnp.int32, (1, TB), 1)
    cum_mask = jnp.logical_and((tcol // RW_CHUNK) == (srow // RW_CHUNK), srow <= tcol)
    lc = _dot_exact_lhs(jnp.where(cum_mask, 1.0, 0.0).astype(BF16), ld)
    n_chunks = TB // RW_CHUNK
    chunks = range(n_chunks)
    lc_last = [lc[(c + 1) * RW_CHUNK - 1:(c + 1) * RW_CHUNK, :] for c in chunks]
    lc_end = jnp.concatenate([jnp.broadcast_to(x, (RW_CHUNK, GROUP_W)) for x in lc_last], axis=0)
    yield
    e_neg = jnp.exp(-lc)
    kap = (kk * jnp.exp(lc - ld)).astype(BF16)
    kt = (k * e_neg).astype(BF16)
    bt = (bvec * e_neg).astype(BF16)
    rt = r * jnp.exp(lc)
    rtb = rt.astype(BF16)
    vb = v.astype(BF16)
    e_end = jnp.exp(lc_end - lc)
    kbar = (k * e_end).astype(BF16)
    bbar = (bvec * e_end).astype(BF16)
    yield

    def bd(xb):
        return jnp.concatenate([xb] * N_H, axis=0) * bd_b

    tloc = lax.broadcasted_iota(jnp.int32, (RW_CHUNK, 1), 0)
    sloc = lax.broadcasted_iota(jnp.int32, (1, GROUP_W), 1) % RW_CHUNK
    strict = sloc < tloc
    incl = sloc <= tloc
    eye_pk = jnp.where(sloc == tloc, 1.0, 0.0)
    sls = [slice(c * RW_CHUNK, (c + 1) * RW_CHUNK) for c in chunks]

    lhs_kr = [jnp.concatenate([kap[sl], rtb[sl]], axis=0) for sl in sls]
    s_k = [_dot(lhs_kr[c], bd(kt[sls[c]]), _NT) for c in chunks]
    s_b = [_dot(lhs_kr[c], bd(bt[sls[c]]), _NT) for c in chunks]
    yield
    nmat = [jnp.where(strict, -s_b[c][0:RW_CHUNK], 0.0) for c in chunks]
    zs = [eye_pk + nmat[c] for c in chunks]
    ypow = [nmat[c].astype(BF16) for c in chunks]
    ypow = [_dot(ypow[c], bd(ypow[c])).astype(BF16) for c in chunks]
    n_iter = int(math.log2(RW_CHUNK)) - 1
    yield
    for it in range(n_iter):
        rhs = [bd(ypow[c]) for c in chunks]
        if it + 1 < n_iter:
            res = [_dot(jnp.concatenate([ypow[c], zs[c].astype(BF16)], axis=0), rhs[c]) for c in chunks]
            ypow = [res[c][0:RW_CHUNK].astype(BF16) for c in chunks]
            zs = [zs[c] + res[c][RW_CHUNK:2 * RW_CHUNK] for c in chunks]
        else:
            zs = [zs[c] + _dot(zs[c].astype(BF16), rhs[c]) for c in chunks]
        yield
    tinv = [zs[c].astype(BF16) for c in chunks]
    lhs_v = [jnp.concatenate([jnp.where(strict, s_k[c][0:RW_CHUNK], 0.0),
                              jnp.where(incl, s_k[c][RW_CHUNK:2 * RW_CHUNK], 0.0)], axis=0).astype(BF16)
             for c in chunks]
    res_v = [_dot(lhs_v[c], bd(vb[sls[c]])) for c in chunks]
    yield
    p1 = [_dot(tinv[c], bd(kap[sls[c]])) for c in chunks]
    p2 = [_dot(tinv[c], bd(res_v[c][0:RW_CHUNK].astype(BF16))) for c in chunks]
    yield
    a_rb = [jnp.where(incl, s_b[c][RW_CHUNK:2 * RW_CHUNK], 0.0).astype(BF16) for c in chunks]
    y0 = [res_v[c][RW_CHUNK:2 * RW_CHUNK] - _dot(a_rb[c], bd(p2[c].astype(BF16))) for c in chunks]
    qq = [rt[sls[c]] - _dot(a_rb[c], bd(p1[c].astype(BF16))) for c in chunks]
    yield

    s_val = s_st[...]
    ys = []
    for c in chunks:
        sl = sls[c]
        lhs = jnp.concatenate([p1[c], qq[c]], axis=0)
        res = _bdot(lhs, s_val, _NT)
        u_c = res[0:RW_CHUNK] + p2[c]
        ys.append(y0[c] + res[RW_CHUNK:2 * RW_CHUNK])
        lt = jnp.concatenate([vb[sl], (-u_c).astype(BF16)], axis=0)
        rt2 = jnp.concatenate([kbar[sl], bbar[sl]], axis=0)
        s_val = s_val * jnp.exp(lc_last[c]) + jnp.where(blockdiag, _dot(lt, rt2, _TN), 0.0)
        yield
    s_st[...] = s_val
    y = jnp.concatenate(ys, axis=0)

    yn = _head_norm(y, gng_ref[...], gnb_ref[...], bd_b)
    yield
    bonus = _head_sum(r * k * rk_ref[...], bd_b) * v
    y_ref[...] = ((yn + bonus) * g).astype(y_ref.dtype)


_P_SEGMENTS = (("pool", 0, 256), ("rw", 256, 1152), ("lru", 1152, 1664), ("ml", 1664, D_IN_PAD))


def _inproj_stages(h_ref, w_ref, p_refs):
    hb = h_ref[...].astype(BF16)
    for name, lo, hi in _P_SEGMENTS:
        for c0 in range(lo, hi, GROUP_W):
            c1 = min(c0 + GROUP_W, hi)
            p_refs[name][:, c0 - lo:c1 - lo] = _dot(hb, w_ref[:, c0:c1])
            yield


def _post_stages(alpha, y_s, h_ref, wo_ref, g1_ref, b1_ref, w1_ref, w2_ref, g2_ref, b2_ref, o_ref,
                 h1_s, hb_s, zb_s, acc_s):
    slabs = [slice(n * GROUP_W, (n + 1) * GROUP_W) for n in range(D_MODEL // GROUP_W)]
    for ns in slabs:
        h1_s[:, ns] = alpha * h_ref[:, ns] + _dot(y_s[...], wo_ref[:, ns])
        yield
    h1 = _layer_norm(h1_s[...], g1_ref[...], b1_ref[...])
    h1_s[...] = h1
    hb_s[...] = h1.astype(BF16)
    yield
    for j in range(D_FF // D_MODEL):
        for ns in slabs:
            z = jnp.maximum(_dot(hb_s[...], w1_ref[:, j * D_MODEL + ns.start:j * D_MODEL + ns.stop]), 0.0)
            zb_s[:, ns] = (z * z).astype(BF16)
            yield
        for ns in slabs:
            upd = _dot(zb_s[...], w2_ref[j * D_MODEL:(j + 1) * D_MODEL, ns])
            acc_s[:, ns] = upd if j == 0 else acc_s[:, ns] + upd
            yield
    o_ref[...] = _layer_norm(alpha * h1_s[...] + acc_s[...], g2_ref[...], b2_ref[...])


def _fused_layer_kernel(parts, alpha, *refs):
    h_next_ref, h_first_ref, h_prev_ref = refs[:3]
    w_ref = refs[3].at[0]
    wo_ref, g1_ref, b1_ref, w1_ref, w2_ref, g2_ref, b2_ref = refs[4:11]
    post_consts = (wo_ref.at[0], g1_ref, b1_ref, w1_ref.at[0], w2_ref.at[0], g2_ref, b2_ref)
    refs = refs[11:]
    n_in = sum(p[3] + p[4] for p in parts)
    n_out = 1 + sum(p[6] for p in parts)
    ins, outs, scr = refs[:n_in], refs[n_in:n_in + n_out], refs[n_in + n_out:]
    p_refs = {name: r for (name, _, _), r in zip(_P_SEGMENTS, scr)}
    y_s = scr[len(_P_SEGMENTS)]
    post_scr = scr[len(_P_SEGMENTS) + 1:len(_P_SEGMENTS) + 5]
    scr = scr[len(_P_SEGMENTS) + 5:]
    h_out_ref, outs = outs[0], outs[1:]

    @pl.when(pl.program_id(0) == 0)
    def _():
        y_s[...] = jnp.zeros(y_s.shape, y_s.dtype)
        for _ in _inproj_stages(h_first_ref.at[0], w_ref, p_refs):
            pass

    for phase in (_INIT, _BODY):
        gens, totals = [], []
        i = o = s = 0
        for fn, n_stages, pnames, nt, nc, y_slots, no, ns in parts:
            args = [p_refs[n] for n in pnames] + [r.at[0] for r in ins[i:i + nt]]
            args += list(ins[i + nt:i + nt + nc])
            args += [y_s.at[:, k * GROUP_W:(k + 1) * GROUP_W] for k in y_slots]
            args += [r.at[0] for r in outs[o:o + no]] + list(scr[s:s + ns])
            gens.append(fn(phase, *args))
            totals.append(n_stages)
            i, o, s = i + nt + nc, o + no, s + ns
        if phase == _BODY:
            gens.append(_inproj_stages(h_next_ref.at[0], w_ref, p_refs))
            totals.append(_INPROJ_STAGES)
            gens.append(_post_stages(alpha, y_s, h_prev_ref.at[0], *post_consts, h_out_ref.at[0], *post_scr))
            totals.append(_POST_STAGES)
        done = [0] * len(gens)

        def advance(j):
            if next(gens[j], _DONE) is _DONE:
                done[j] = None
            else:
                done[j] += 1

        for _ in range(_LEAD_STAGES):
            advance(0)
        for j in range(len(gens)):
            if done[j] == 0:
                advance(j)
        while any(d is not None for d in done):
            j = min((d / totals[k], k) for k, d in enumerate(done) if d is not None)[1]
            advance(j)


def _layer_call(name, layer, batch, tp, alpha, h, w_in_b, post_consts, parts, drop_front):
    n_t = tp // TB
    n_blk = batch * n_t

    def tspec(width, index):
        return pl.BlockSpec((1, TB, width), index)

    def cspec(shape):
        return pl.BlockSpec(shape, lambda s: (0,) * len(shape), pipeline_mode=pl.Buffered(1))

    def wspec(a):
        if a.ndim == 3:
            return pl.BlockSpec((1,) + a.shape[1:], lambda s: (layer, 0, 0), pipeline_mode=pl.Buffered(1))
        return cspec(a.shape)

    def out_index(s):
        blk = jnp.maximum(s - 1, 0)
        if not drop_front:
            return (blk, 0, 0)
        return ((blk // n_t) * (n_t - 1) + jnp.maximum(blk % n_t - 1, 0), 0, 0)

    in_specs = [tspec(D_MODEL, lambda s: (jnp.minimum(s + 1, n_blk - 1), 0, 0)),
                tspec(D_MODEL, lambda s: (0, 0, 0)),
                tspec(D_MODEL, lambda s: (jnp.maximum(s - 1, 0), 0, 0)),
                wspec(w_in_b)] + [wspec(a) for a in post_consts]
    operands = [h, h, h, w_in_b] + list(post_consts)
    out_shape = [jax.ShapeDtypeStruct((batch * (n_t - 1) if drop_front else n_blk, TB, D_MODEL), F32)]
    out_specs = [tspec(D_MODEL, out_index)]
    scratch = [pltpu.VMEM((TB, hi - lo), F32) for _, lo, hi in _P_SEGMENTS]
    scratch += [pltpu.VMEM((TB, D_MODEL), BF16), pltpu.VMEM((TB, D_MODEL), F32), pltpu.VMEM((TB, D_MODEL), BF16),
                pltpu.VMEM((TB, D_MODEL), BF16), pltpu.VMEM((TB, D_MODEL), F32)]
    sig = []
    for fn, n_stages, pnames, time_ins, const_ins, y_slots, extra_widths, scr in parts:
        in_specs += [tspec(a.shape[2], lambda s: (jnp.minimum(s, n_blk - 1), 0, 0)) for a in time_ins]
        in_specs += [cspec(a.shape) for a in const_ins]
        operands += list(time_ins) + list(const_ins)
        out_shape += [jax.ShapeDtypeStruct((n_blk + 1, TB, w), F32) for w in extra_widths]
        out_specs += [tspec(w, lambda s: (s, 0, 0)) for w in extra_widths]
        scratch += [pltpu.VMEM(tuple(shape), F32) for shape in scr]
        sig.append((functools.partial(fn, n_t), n_stages, tuple(pnames), len(time_ins), len(const_ins),
                    tuple(y_slots), len(extra_widths), len(scr)))
    return pl.pallas_call(
        functools.partial(_fused_layer_kernel, tuple(sig), alpha),
        grid=(n_blk + 1,),
        in_specs=in_specs,
        out_specs=out_specs,
        out_shape=out_shape,
        scratch_shapes=scratch,
        compiler_params=pltpu.CompilerParams(dimension_semantics=("arbitrary",),
                                             vmem_limit_bytes=VMEM_LIMIT),
        name=name,
    )(*operands)


def _block_diag(w):
    eye = jnp.eye(N_H, dtype=w.dtype)
    return jnp.einsum('gcd,gh->gchd', w, eye).reshape(GROUP_W, GROUP_W)


def _pad_rows(w, total, offset):
    return jnp.zeros((total, w.shape[1]), w.dtype).at[offset:offset + w.shape[0]].set(w)


def _row(v):
    return v.reshape(1, -1)


@jax.jit
def kernel(x, meta, emb_ln_g, emb_ln_b, w_in, w_out, pool_w, pool_scale, rw_mu, rw_w0, rw_w_up, rw_a0, rw_a_up, rw_g_up, rw_k_k, rw_k_a, rw_r_k, rw_gn_g, rw_gn_b, rw_v0, rw_v_down, rw_v_up, lru_conv_w, lru_conv_b, lru_ga_w, lru_ga_b, lru_gx_w, lru_gx_b, lru_lambda, ml_if_b, ml_gn_g, ml_gn_b, ln1_g, ln1_b, ln2_g, ln2_b, mlp_w1, mlp_w2):
    batch, seq, _ = x.shape
    depth = w_in.shape[0]
    assert seq % TB == 0 and x.shape[2] == D_MODEL
    tp = TB + seq
    alpha = (2 * depth) ** 0.25

    h = pl.pallas_call(
        _embed_ln_kernel,
        grid=(tp // TB,),
        in_specs=[pl.BlockSpec((batch, TB, D_MODEL), lambda t: (0, jnp.maximum(t - 1, 0), 0)),
                  pl.BlockSpec((N_META, D_MODEL), lambda t: (0, 0)),
                  pl.BlockSpec((1, D_MODEL), lambda t: (0, 0)),
                  pl.BlockSpec((1, D_MODEL), lambda t: (0, 0))],
        out_specs=pl.BlockSpec((batch, TB, D_MODEL), lambda t: (0, t, 0)),
        out_shape=jax.ShapeDtypeStruct((batch, tp, D_MODEL), F32),
        compiler_params=pltpu.CompilerParams(dimension_semantics=("arbitrary",)),
        name="embed_ln",
    )(x, meta.astype(x.dtype), _row(emb_ln_g), _row(emb_ln_b)).reshape(batch * (tp // TB), TB, D_MODEL)

    w_in_b = jnp.pad(w_in.astype(BF16), ((0, 0), (0, 0), (0, D_IN_PAD - w_in.shape[2])))
    w_out_b = w_out.astype(BF16)
    w1_b = mlp_w1.astype(BF16)
    w2_b = mlp_w2.astype(BF16)

    v_first = None
    for l in range(depth):
        wax = jnp.concatenate([_block_diag(lru_ga_w[l]), _block_diag(lru_gx_w[l])], axis=1).astype(BF16)
        bax = jnp.concatenate([lru_ga_b[l], lru_gx_b[l]]).reshape(1, 2 * GROUP_W)
        pool_lru_part = (
            _pool_lru_kernel, _POOL_LRU_STAGES, ("pool", "lru"), [],
            [_block_diag(pool_w[l]).astype(BF16), _row(pool_scale[l]), lru_conv_w[l], _row(lru_conv_b[l]),
             wax, bax, _row(lru_lambda[l])],
            (0, 2), [],
            [(TB + _POOL_OFF, GROUP_W)] * 4 + [(TB + SUBLANE, GROUP_W), (TB, GROUP_W),
                                               (TB + _SCAN_PAD, GROUP_W), (TB + _SCAN_PAD, GROUP_W),
                                               (1, GROUP_W)])

        ifb = jnp.zeros((1, LANE), F32).at[0, 0:2 * N_H].set(ml_if_b[l])
        mlstm_part = (
            _mlstm_kernel, _MLSTM_STAGES, ("ml",), [], [ifb, _row(ml_gn_g[l]), _row(ml_gn_b[l])], (3,), [],
            [(GROUP_W, GROUP_W), (1, GROUP_W), (1, LANE)])

        wup = _pad_rows(rw_w_up[l], LANE, 0).astype(BF16)
        aup = _pad_rows(rw_a_up[l], LANE, 32).astype(BF16)
        gup = _pad_rows(rw_g_up[l], LANE, 64).astype(BF16)
        rw_consts = [_row(rw_mu[l]), _row(rw_w0[l]), wup, _row(rw_a0[l]), aup, gup, _row(rw_k_k[l]),
                     _row(rw_k_a[l]), _row(rw_r_k[l]), _row(rw_gn_g[l]), _row(rw_gn_b[l])]
        rw_scratch = [(TB + SUBLANE, RW_COLS), (GROUP_W, GROUP_W)]
        post_consts = [w_out_b, _row(ln1_g[l]), _row(ln1_b[l]), w1_b, w2_b, _row(ln2_g[l]), _row(ln2_b[l])]
        if l == 0:
            rwkv_part = (functools.partial(_rwkv_kernel, False), _RWKV_STAGES, ("rw",), [], rw_consts,
                         (1,), [GROUP_W], rw_scratch)
            h, v_first = _layer_call("layer_first", l, batch, tp, alpha, h, w_in_b, post_consts,
                                     [rwkv_part, mlstm_part, pool_lru_part], l == depth - 1)
        else:
            vdn = jnp.pad(rw_v_down[l - 1], ((0, 0), (0, LANE - rw_v_down.shape[2]))).astype(BF16)
            vup = _pad_rows(rw_v_up[l - 1], LANE, 0).astype(BF16)
            rwkv_part = (functools.partial(_rwkv_kernel, True), _RWKV_STAGES, ("rw",), [v_first],
                         rw_consts + [_row(rw_v0[l - 1]), vdn, vup], (1,), [], rw_scratch)
            h, = _layer_call("layer", l, batch, tp, alpha, h, w_in_b, post_consts,
                             [rwkv_part, mlstm_part, pool_lru_part], l == depth - 1)

    return h.reshape(batch, seq, D_MODEL)
```
